```python
import math
import jax, jax.numpy as jnp
from jax import lax
import numpy as np

D_MODEL = 1024
BATCH = 8
SEQ = 8192
DEPTH = 2

CHUNK = 64
N_META = 16
SSM_HEAD_DIM = 64
SSM_INNER = D_MODEL
SSM_HEADS = SSM_INNER // SSM_HEAD_DIM
SSM_GROUPS = 2
SSM_STATE = 128
SSM_CONV = 4
SSM_CONV_DIM = SSM_INNER + 2 * SSM_GROUPS * SSM_STATE
RWKV_HEAD_DIM = 64
RWKV_DIM = D_MODEL
RWKV_HEADS = RWKV_DIM // RWKV_HEAD_DIM
RWKV_DECAY_LORA = 64
RWKV_AAA_LORA = 64
RWKV_GATE_LORA = 128
RWKV_COLS = 3 * RWKV_DIM + RWKV_DECAY_LORA + RWKV_AAA_LORA + RWKV_GATE_LORA
RWKV_LN_EPS = 64e-5
IN0 = SSM_INNER + SSM_CONV_DIM + SSM_HEADS + RWKV_COLS
MIX0 = SSM_INNER + RWKV_DIM
LRU_DIM = D_MODEL
LRU_BLOCKS = 8
LRU_BLOCK = LRU_DIM // LRU_BLOCKS
LRU_CONV = 4
LRU_C = 8.0
D_FF = 2816
N_EXPERTS = 8
TOP_K = 2
MOE_BLOCK = 128
N_EVEN = (DEPTH + 1) // 2
N_ODD = DEPTH // 2
DEEPNORM_ALPHA = (2 * DEPTH) ** 0.25
DEEPNORM_BETA = (8 * DEPTH) ** -0.25
LN_EPS = 1e-5

kernel_name = 'hybrid_ssd_rwkv7_rglru_moe_deepnorm'


def layer_norm(x, g, b):
    xf = x.astype(jnp.float32)
    mu = jnp.mean(xf, -1, keepdims=True)
    var = jnp.mean(jnp.square(xf - mu), -1, keepdims=True)
    return ((xf - mu) * lax.rsqrt(var + LN_EPS) * g + b).astype(x.dtype)


def causal_dwconv(x, w, b):
    k = w.shape[0]
    y = lax.conv_general_dilated(x, w[:, None, :].astype(x.dtype), (1,), [(k - 1, 0)],
                                 dimension_numbers=('NWC', 'WIO', 'NWC'),
                                 feature_group_count=x.shape[-1])
    return y + b


def swiglu(x, w_gu, w_down):
    g, u = jnp.split(x @ w_gu, 2, axis=-1)
    return (jax.nn.silu(g) * u) @ w_down


def ssd_chunked(x, dt, A, Bm, Cm):
    b, l, g, e, p = x.shape
    n = Bm.shape[-1]
    c, q = l // CHUNK, CHUNK
    xdt = (x * dt[..., None]).reshape(b, c, q, g, e, p)
    Bc = Bm.reshape(b, c, q, g, n)
    Cc = Cm.reshape(b, c, q, g, n)
    acs = jnp.cumsum(jnp.moveaxis((dt * A).reshape(b, c, q, g, e), 2, -1), axis=-1)
    causal = jnp.tril(jnp.ones((q, q), bool))
    decay_in = jnp.exp(jnp.where(causal, acs[..., :, None] - acs[..., None, :], -jnp.inf))
    scores = jnp.einsum('bclgn,bcsgn->bcgls', Cc, Bc)
    y_diag = jnp.einsum('bcgels,bcsgep->bclgep', scores[:, :, :, None] * decay_in, xdt)
    to_end = jnp.moveaxis(jnp.exp(acs[..., -1:] - acs), -1, 2)[..., None]
    states = jnp.einsum('bcsgn,bcsgep->bcgepn', Bc, xdt * to_end)
    chunk_decay = jnp.exp(acs[..., -1])

    def carry_state(h, inp):
        s, d = inp
        return h * d[..., None, None] + s, h

    _, prev = lax.scan(carry_state, jnp.zeros_like(states[:, 0]),
                       (jnp.moveaxis(states, 1, 0), jnp.moveaxis(chunk_decay, 1, 0)))
    prev = jnp.moveaxis(prev, 0, 1)
    y_off = jnp.einsum('bclgn,bcgepn->bclgep', Cc, prev) * jnp.moveaxis(jnp.exp(acs), -1, 2)[..., None]
    return (y_diag + y_off).reshape(b, l, g, e, p)


def ssd_branch(z, xbc, dt_raw, conv_w, conv_b, dt_bias, a_log, d_skip, norm_w):
    b, l, _ = z.shape
    pad = (-l) % CHUNK
    lp = l + pad
    xbc = jax.nn.silu(causal_dwconv(jnp.pad(xbc, ((0, 0), (pad, 0), (0, 0))), conv_w, conv_b))
    xs, bm, cm = jnp.split(xbc.astype(jnp.float32), [SSM_INNER, SSM_INNER + SSM_GROUPS * SSM_STATE], axis=-1)
    valid = (jnp.arange(lp) >= pad).astype(jnp.float32)[None, :, None]
    dt = jax.nn.softplus(jnp.pad(dt_raw.astype(jnp.float32), ((0, 0), (pad, 0), (0, 0))) + dt_bias) * valid
    g, e = SSM_GROUPS, SSM_HEADS // SSM_GROUPS
    xs = xs.reshape(b, lp, g, e, SSM_HEAD_DIM)
    A = -jnp.exp(a_log.astype(jnp.float32)).reshape(g, e)
    y = ssd_chunked(xs, dt.reshape(b, lp, g, e), A,
                    bm.reshape(b, lp, g, SSM_STATE), cm.reshape(b, lp, g, SSM_STATE))
    y = (y + d_skip.reshape(g, e, 1) * xs)[:, pad:].reshape(b, l, g, e * SSM_HEAD_DIM)
    y = y * jax.nn.silu(z.astype(jnp.float32)).reshape(b, l, g, e * SSM_HEAD_DIM)
    y = y * lax.rsqrt(jnp.mean(y * y, -1, keepdims=True) + LN_EPS) * norm_w.reshape(g, -1)
    return y.reshape(b, l, SSM_INNER).astype(z.dtype)


def rwkv7_recurrence(r, decay, k, v, kk, a):
    def step(S, inp):
        r_t, w_t, k_t, v_t, kk_t, a_t = inp
        sa = jnp.einsum('bhvk,bhk->bhv', S, kk_t)
        S = S * w_t[:, :, None, :] - sa[..., None] * (kk_t * a_t)[:, :, None, :] + v_t[..., None] * k_t[:, :, None, :]
        return S, jnp.einsum('bhvk,bhk->bhv', S, r_t)

    b, l, h, n = r.shape
    seq = tuple(jnp.moveaxis(t, 1, 0) for t in (r, decay, k, v, kk, a))
    _, out = lax.scan(step, jnp.zeros((b, h, n, n), jnp.float32), seq)
    return jnp.moveaxis(out, 0, 1)


def rwkv7_branch(cols, shift_mu, w0, w_up, a0, a_up, g_up, k_k, k_a, r_k, lnx_g, lnx_b):
    b, l, _ = cols.shape
    hs = (RWKV_HEADS, RWKV_HEAD_DIM)
    prev = jnp.pad(cols, ((0, 0), (1, 0), (0, 0)))[:, :l]
    mixed = (cols + (prev - cols) * shift_mu).astype(jnp.float32)
    s1, s2, s3 = RWKV_DIM, 2 * RWKV_DIM, 3 * RWKV_DIM
    s4 = s3 + RWKV_DECAY_LORA
    s5 = s4 + RWKV_AAA_LORA
    r, k, v, w_lo, a_lo, g_lo = jnp.split(mixed, [s1, s2, s3, s4, s5], axis=-1)
    w = -jax.nn.softplus(-(w0 + jnp.tanh(w_lo) @ w_up)) - 0.5
    decay = jnp.exp(-jnp.exp(w))
    a = jax.nn.sigmoid(a0 + a_lo @ a_up)
    gate = jax.nn.sigmoid(g_lo) @ g_up
    r, k, v, decay, a = (t.reshape(b, l, *hs) for t in (r, k, v, decay, a))
    kk = k * k_k.reshape(hs)
    kk = kk * lax.rsqrt(jnp.maximum(jnp.sum(kk * kk, -1, keepdims=True), 1e-24))
    k = k * (1.0 + (a - 1.0) * k_a.reshape(hs))
    o = rwkv7_recurrence(r, decay, k, v, kk, a)
    mu = jnp.mean(o, -1, keepdims=True)
    var = jnp.mean(jnp.square(o - mu), -1, keepdims=True)
    o = (o - mu) * lax.rsqrt(var + RWKV_LN_EPS) * lnx_g.reshape(hs) + lnx_b.reshape(hs)
    o = o + jnp.sum(r * k * r_k, -1, keepdims=True) * v
    return (o.reshape(b, l, RWKV_DIM) * gate).astype(cols.dtype)


def ssd_rwkv_mixer(h, w_in, conv_w, conv_b, dt_bias, a_log, d_skip, ssm_norm, shift_mu, w0, w_up,
                   a0, a_up, g_up, k_k, k_a, r_k, lnx_g, lnx_b, w_out):
    proj = h @ w_in
    z, xbc, dt_raw, cols = jnp.split(
        proj, [SSM_INNER, SSM_INNER + SSM_CONV_DIM, SSM_INNER + SSM_CONV_DIM + SSM_HEADS], axis=-1)
    y_a = ssd_branch(z, xbc, dt_raw, conv_w, conv_b, dt_bias, a_log, d_skip, ssm_norm)
    y_b = rwkv7_branch(cols, shift_mu, w0, w_up, a0, a_up, g_up, k_k, k_a, r_k, lnx_g, lnx_b)
    return jnp.concatenate([y_a, y_b], axis=-1) @ w_out


def rglru_block(h, w_in, conv_w, conv_b, gx_w, gx_b, ga_w, ga_b, lam, w_out):
    b, l, _ = h.shape
    gate_branch, xr = jnp.split(h @ w_in, 2, axis=-1)
    xf = causal_dwconv(xr, conv_w, conv_b).astype(jnp.float32)
    xb = xf.reshape(b, l, LRU_BLOCKS, LRU_BLOCK)
    gate_x = jax.nn.sigmoid(jnp.einsum('blhi,hij->blhj', xb, gx_w).reshape(b, l, LRU_DIM) + gx_b)
    gate_a = jax.nn.sigmoid(jnp.einsum('blhi,hij->blhj', xb, ga_w).reshape(b, l, LRU_DIM) + ga_b)
    log_a = -LRU_C * gate_a * jax.nn.softplus(-lam)
    a = jnp.exp(log_a)
    u = jnp.sqrt(-jnp.expm1(2.0 * log_a)) * (gate_x * xf)

    def combine(lhs, rhs):
        a1, b1 = lhs
        a2, b2 = rhs
        return a1 * a2, a2 * b1 + b2

    _, hs = lax.associative_scan(combine, (a, u), axis=1)
    y = hs.astype(h.dtype) * jax.nn.gelu(gate_branch)
    return y @ w_out


def moe_swiglu(x, w_router, w_gu, w_down):
    b, l, d = x.shape
    n_tok = b * l
    n_slot = n_tok * TOP_K
    xt = x.reshape(n_tok, d)
    logits = (xt @ w_router).astype(jnp.float32)
    top_logit, top_idx = lax.top_k(logits, TOP_K)
    top_gate = jax.nn.softmax(top_logit, axis=-1)
    slot_e = top_idx.reshape(n_slot)
    order = jnp.argsort(slot_e, stable=True)
    sorted_e = slot_e[order]
    sorted_tok = order // TOP_K
    sorted_gate = top_gate.reshape(n_slot)[order]
    counts = jnp.bincount(slot_e, length=N_EXPERTS)
    starts = jnp.cumsum(counts) - counts
    padded = (counts + MOE_BLOCK - 1) // MOE_BLOCK * MOE_BLOCK
    padded_end = jnp.cumsum(padded)
    dest = (padded_end - padded)[sorted_e] + jnp.arange(n_slot) - starts[sorted_e]
    n_blocks = -(-(n_slot + N_EXPERTS * (MOE_BLOCK - 1)) // MOE_BLOCK)
    buf = jnp.zeros((n_blocks * MOE_BLOCK, d), x.dtype).at[dest].set(xt[sorted_tok])
    block_e = jnp.minimum(jnp.searchsorted(padded_end, jnp.arange(n_blocks) * MOE_BLOCK, side='right'),
                          N_EXPERTS - 1)

    def expert_block(args):
        xb, e = args
        return swiglu(xb, w_gu[e], w_down[e])

    out = lax.map(expert_block, (buf.reshape(n_blocks, MOE_BLOCK, d), block_e)).reshape(n_blocks * MOE_BLOCK, d)
    y = jnp.zeros((n_tok, d), x.dtype).at[sorted_tok].add(out[dest] * sorted_gate[:, None].astype(x.dtype))
    return y.reshape(b, l, d)


def setup_inputs(seed: int = 0) -> dict:
    key = jax.random.key(seed)
    keys = jax.random.split(key, 64)
    counter = [0]
    f32 = jnp.float32

    def nxt():
        k = keys[counter[0]]
        counter[0] += 1
        return k

    def nrm(shape, scale):
        return jax.random.normal(nxt(), shape, f32) * scale

    def unif(shape, lo, hi):
        return jax.random.uniform(nxt(), shape, f32, lo, hi)

    def gain(shape):
        return 1.0 + nrm(shape, 0.05)

    NE, NO = N_EVEN, N_ODD
    dt0 = jnp.exp(unif((NE, SSM_HEADS), math.log(1e-3), math.log(1e-1)))
    a_root = unif((NO, LRU_DIM), 0.9, 0.999) ** (1.0 / LRU_C)
    return {
        'x': nrm((BATCH, SEQ, D_MODEL), 1.0),
        'meta': nrm((N_META, D_MODEL), 1.0),
        'ev_w_in': nrm((NE, D_MODEL, IN0), D_MODEL ** -0.5),
        'ev_conv_w': nrm((NE, SSM_CONV, SSM_CONV_DIM), 0.5),
        'ev_conv_b': nrm((NE, SSM_CONV_DIM), 0.1),
        'ev_dt_bias': dt0 + jnp.log(-jnp.expm1(-dt0)),
        'ev_a_log': jnp.log(unif((NE, SSM_HEADS), 1.0, 16.0)),
        'ev_d_skip': 1.0 + nrm((NE, SSM_HEADS), 0.1),
        'ev_ssm_norm': gain((NE, SSM_INNER)),
        'ev_shift_mu': unif((NE, RWKV_COLS), 0.0, 1.0),
        'ev_w0': unif((NE, RWKV_DIM), -6.0, -1.0),
        'ev_w_up': nrm((NE, RWKV_DECAY_LORA, RWKV_DIM), 0.5 * RWKV_DECAY_LORA ** -0.5),
        'ev_a0': nrm((NE, RWKV_DIM), 0.5),
        'ev_a_up': nrm((NE, RWKV_AAA_LORA, RWKV_DIM), 0.5 * RWKV_AAA_LORA ** -0.5),
        'ev_g_up': nrm((NE, RWKV_GATE_LORA, RWKV_DIM), RWKV_GATE_LORA ** -0.5),
        'ev_k_k': 0.85 + nrm((NE, RWKV_DIM), 0.05),
        'ev_k_a': gain((NE, RWKV_DIM)),
        'ev_r_k': nrm((NE, RWKV_HEADS, RWKV_HEAD_DIM), 0.1),
        'ev_lnx_g': gain((NE, RWKV_DIM)),
        'ev_lnx_b': nrm((NE, RWKV_DIM), 0.02),
        'ev_w_out': nrm((NE, MIX0, D_MODEL), DEEPNORM_BETA * MIX0 ** -0.5),
        'ev_ln1_g': gain((NE, D_MODEL)),
        'ev_ln1_b': nrm((NE, D_MODEL), 0.02),
        'ev_ffn_w_gu': nrm((NE, D_MODEL, 2 * D_FF), D_MODEL ** -0.5),
        'ev_ffn_w_down': nrm((NE, D_FF, D_MODEL), DEEPNORM_BETA * D_FF ** -0.5),
        'ev_ln2_g': gain((NE, D_MODEL)),
        'ev_ln2_b': nrm((NE, D_MODEL), 0.02),
        'od_w_in': nrm((NO, D_MODEL, 2 * LRU_DIM), D_MODEL ** -0.5),
        'od_conv_w': nrm((NO, LRU_CONV, LRU_DIM), 0.5),
        'od_conv_b': nrm((NO, LRU_DIM), 0.1),
        'od_gx_w': nrm((NO, LRU_BLOCKS, LRU_BLOCK, LRU_BLOCK), LRU_BLOCK ** -0.5),
        'od_gx_b': nrm((NO, LRU_DIM), 0.1),
        'od_ga_w': nrm((NO, LRU_BLOCKS, LRU_BLOCK, LRU_BLOCK), LRU_BLOCK ** -0.5),
        'od_ga_b': nrm((NO, LRU_DIM), 0.1),
        'od_lambda': jnp.log(a_root) - jnp.log1p(-a_root),
        'od_w_out': nrm((NO, LRU_DIM, D_MODEL), DEEPNORM_BETA * LRU_DIM ** -0.5),
        'od_ln1_g': gain((NO, D_MODEL)),
        'od_ln1_b': nrm((NO, D_MODEL), 0.02),
        'od_router': nrm((NO, D_MODEL, N_EXPERTS), D_MODEL ** -0.5),
        'od_exp_w_gu': nrm((NO, N_EXPERTS, D_MODEL, 2 * D_FF), D_MODEL ** -0.5),
        'od_exp_w_down': nrm((NO, N_EXPERTS, D_FF, D_MODEL), DEEPNORM_BETA * D_FF ** -0.5),
        'od_ln2_g': gain((NO, D_MODEL)),
        'od_ln2_b': nrm((NO, D_MODEL), 0.02),
    }


def reference(x, meta, ev_w_in, ev_conv_w, ev_conv_b, ev_dt_bias, ev_a_log, ev_d_skip, ev_ssm_norm,
              ev_shift_mu, ev_w0, ev_w_up, ev_a0, ev_a_up, ev_g_up, ev_k_k, ev_k_a, ev_r_k, ev_lnx_g,
              ev_lnx_b, ev_w_out, ev_ln1_g, ev_ln1_b, ev_ffn_w_gu, ev_ffn_w_down, ev_ln2_g, ev_ln2_b,
              od_w_in, od_conv_w, od_conv_b, od_gx_w, od_gx_b, od_ga_w, od_ga_b, od_lambda, od_w_out,
              od_ln1_g, od_ln1_b, od_router, od_exp_w_gu, od_exp_w_down, od_ln2_g, od_ln2_b):
    b = x.shape[0]
    h = jnp.concatenate([jnp.broadcast_to(meta.astype(x.dtype)[None], (b, N_META, D_MODEL)), x], axis=1)
    for layer in range(DEPTH):
        i = layer // 2
        if layer % 2 == 0:
            mix = ssd_rwkv_mixer(h, ev_w_in[i], ev_conv_w[i], ev_conv_b[i], ev_dt_bias[i], ev_a_log[i],
                                 ev_d_skip[i], ev_ssm_norm[i], ev_shift_mu[i], ev_w0[i], ev_w_up[i],
                                 ev_a0[i], ev_a_up[i], ev_g_up[i], ev_k_k[i], ev_k_a[i], ev_r_k[i],
                                 ev_lnx_g[i], ev_lnx_b[i], ev_w_out[i])
            h = layer_norm(DEEPNORM_ALPHA * h + mix, ev_ln1_g[i], ev_ln1_b[i])
            h = layer_norm(DEEPNORM_ALPHA * h + swiglu(h, ev_ffn_w_gu[i], ev_ffn_w_down[i]),
                           ev_ln2_g[i], ev_ln2_b[i])
        else:
            mix = rglru_block(h, od_w_in[i], od_conv_w[i], od_conv_b[i], od_gx_w[i], od_gx_b[i],
                              od_ga_w[i], od_ga_b[i], od_lambda[i], od_w_out[i])
            h = layer_norm(DEEPNORM_ALPHA * h + mix, od_ln1_g[i], od_ln1_b[i])
            h = layer_norm(DEEPNORM_ALPHA * h + moe_swiglu(h, od_router[i], od_exp_w_gu[i], od_exp_w_down[i]),
                           od_ln2_g[i], od_ln2_b[i])
    return h[:, N_META:]
```

```python
import functools
import math

import jax
import jax.numpy as jnp
from jax import lax
from jax.experimental import pallas as pl
from jax.experimental.pallas import tpu as pltpu

F32 = jnp.float32
BF16 = jnp.bfloat16
HIGHEST = lax.Precision.HIGHEST

D_MODEL = 1024
N_META = 16
CHUNK = 64
HEAD_DIM = 64
N_HEADS = D_MODEL // HEAD_DIM
SSM_GROUPS = 2
SSM_STATE = 128
SSM_CONV_DIM = D_MODEL + 2 * SSM_GROUPS * SSM_STATE
CONV_K = 4
RWKV_LORA = 256
RWKV_COLS = 3 * D_MODEL + RWKV_LORA
RWKV_LN_EPS = 64e-5
LRU_BLOCKS = 8
LRU_BLOCK = D_MODEL // LRU_BLOCKS
LRU_C = 8.0
D_FF = 2816
N_EXPERTS = 8
DEPTH = 2
DEEPNORM_ALPHA = (2 * DEPTH) ** 0.25
LN_EPS = 1e-5

LANE = 128
HALO = 8
GROUP_HEADS = 4
GROUP_W = GROUP_HEADS * HEAD_DIM
TIME_BLOCK = 192
FF_TILE = 256
VMEM_LIMIT = 56 * 1024 * 1024


def _cparams(*sem):
    return pltpu.CompilerParams(dimension_semantics=sem, vmem_limit_bytes=VMEM_LIMIT)


def _row_tile(lp):
    best = 16
    for t in range(16, min(lp, 1376) + 1, 16):
        if lp % t == 0:
            best = t
    return best


def _col_tile(n):
    for t in (512, 1664, 256, 128):
        if n % t == 0:
            return t
    raise ValueError(n)


def _sigmoid(x):
    return 1.0 / (1.0 + jnp.exp(-x))


def _silu(x):
    return x * _sigmoid(x)


def _softplus(x):
    return jnp.maximum(x, 0.0) + jnp.log(1.0 + jnp.exp(-jnp.abs(x)))


def _dot(a, b):
    return jnp.dot(a, b, preferred_element_type=F32)


def _dot_nt(a, b):
    return lax.dot_general(a, b, (((1,), (1,)), ((), ())), preferred_element_type=F32)


def _dot_tn(a, b):
    return lax.dot_general(a, b, (((0,), (0,)), ((), ())), preferred_element_type=F32)


def _dot_exact(a, b):
    return jnp.dot(a, b, preferred_element_type=F32, precision=HIGHEST)


def _layer_norm(x, g, b):
    mu = jnp.mean(x, axis=-1, keepdims=True)
    xc = x - mu
    var = jnp.mean(xc * xc, axis=-1, keepdims=True)
    return xc * lax.rsqrt(var + LN_EPS) * g + b


def _mm_body(x_ref, w_ref, o_ref):
    o_ref[0] = _dot(x_ref[0], w_ref[...]).astype(o_ref.dtype)


def _mm(x, w, out_dtype, tm):
    b, lp, k = x.shape
    n = w.shape[1]
    tn = _col_tile(n)
    return pl.pallas_call(
        _mm_body,
        grid=(b, lp // tm, n // tn),
        in_specs=[pl.BlockSpec((1, tm, k), lambda b_, i, j: (b_, i, 0)),
                  pl.BlockSpec((k, tn), lambda b_, i, j: (0, j))],
        out_specs=pl.BlockSpec((1, tm, tn), lambda b_, i, j: (b_, i, j)),
        out_shape=jax.ShapeDtypeStruct((b, lp, n), out_dtype),
        compiler_params=_cparams("parallel", "parallel", "arbitrary"),
        name="dense_matmul",
    )(x, w)


def _chunk_consts():
    row = lax.broadcasted_iota(jnp.int32, (CHUNK, GROUP_W), 0)
    col = lax.broadcasted_iota(jnp.int32, (CHUNK, GROUP_W), 1) % HEAD_DIM
    r2 = lax.broadcasted_iota(jnp.int32, (GROUP_W, GROUP_W), 0) // HEAD_DIM
    c2 = lax.broadcasted_iota(jnp.int32, (GROUP_W, GROUP_W), 1) // HEAD_DIM
    tri_r = lax.broadcasted_iota(jnp.int32, (CHUNK, CHUNK), 0)
    tri_c = lax.broadcasted_iota(jnp.int32, (CHUNK, CHUNK), 1)
    tri = (tri_c <= tri_r).astype(F32)
    return row, col, (r2 == c2), tri


def _block_diag(y, bd_mask):
    return jnp.where(bd_mask, jnp.concatenate([y] * GROUP_HEADS, axis=0), jnp.zeros((), y.dtype))


def _shift_halo(ext_ref, blk, first):
    tb = blk.shape[0]

    @pl.when(first)
    def _():
        ext_ref[0:HALO, :] = jnp.zeros((HALO, ext_ref.shape[1]), F32)

    @pl.when(jnp.logical_not(first))
    def _():
        ext_ref[0:HALO, :] = ext_ref[tb:tb + HALO, :]

    ext_ref[HALO:HALO + tb, :] = blk


def _ssd_body(pad, z_ref, xbc_ref, dtx_ref, cw_ref, cb_ref, dtb_ref, alog_ref, dsk_ref, nw_ref,
              y_ref, ext_ref, xc_ref, dt_ref, yb_ref, h_ref):
    tb = xbc_ref.shape[1]
    j = pl.program_id(1)

    @pl.when(j == 0)
    def _():
        h_ref[...] = jnp.zeros(h_ref.shape, F32)

    _shift_halo(ext_ref, xbc_ref[0], j == 0)
    acc = jnp.broadcast_to(cb_ref[...], (tb, SSM_CONV_DIM))
    for k in range(CONV_K):
        acc = acc + cw_ref[k:k + 1, :] * ext_ref[pl.ds(HALO - (CONV_K - 1) + k, tb), :]
    xc_ref[...] = _silu(acc)

    pos = j * tb + lax.broadcasted_iota(jnp.int32, (tb, 1), 0)
    dt_ref[...] = jnp.where(pos >= pad, _softplus(dtx_ref[0] + dtb_ref[...]), 0.0)

    row, col, bd_mask, tri = _chunk_consts()
    row_w = jnp.concatenate([row] * (D_MODEL // GROUP_W), axis=1)
    col_w = jnp.concatenate([col] * (D_MODEL // GROUP_W), axis=1)
    a_neg = -jnp.exp(alog_ref[...])
    gw = D_MODEL // SSM_GROUPS

    def chunk(c, carry):
        rows = pl.ds(pl.multiple_of(c * CHUNK, CHUNK), CHUNK)
        xs = xc_ref[rows, 0:D_MODEL]
        dt = dt_ref[rows, :]
        xdt = xs * dt
        acs = _dot_exact(tri, dt * a_neg)
        acs_t = jnp.sum(jnp.where(row_w == col_w, acs, 0.0), axis=0, keepdims=True)
        decay = jnp.where(row_w >= col_w, jnp.exp(jnp.minimum(acs - acs_t, 0.0)), 0.0)
        e_acs = jnp.exp(acs)
        last = acs[CHUNK - 1:CHUNK, :]
        xend = xdt * jnp.exp(last - acs)
        e_last = jnp.exp(last)
        for g in range(SSM_GROUPS):
            bg = xc_ref[rows, D_MODEL + g * SSM_STATE:D_MODEL + (g + 1) * SSM_STATE].astype(BF16)
            cg = xc_ref[rows, D_MODEL + (SSM_GROUPS + g) * SSM_STATE:
                        D_MODEL + (SSM_GROUPS + g + 1) * SSM_STATE].astype(BF16)
            scores = _dot_nt(cg, jnp.concatenate([bg] * GROUP_HEADS, axis=0))
            hg = h_ref[:, g * gw:(g + 1) * gw]
            y_off = _dot(cg, hg.astype(BF16)) * e_acs[:, g * gw:(g + 1) * gw]
            for t in range(gw // GROUP_W):
                lo = g * gw + t * GROUP_W
                m = (scores * decay[:, lo:lo + GROUP_W]).astype(BF16)
                xbd = _block_diag(xdt[:, lo:lo + GROUP_W].astype(BF16), bd_mask)
                yb_ref[rows, lo:lo + GROUP_W] = (_dot(m, xbd) + y_off[:, t * GROUP_W:(t + 1) * GROUP_W])
            st = _dot_tn(bg, xend[:, g * gw:(g + 1) * gw].astype(BF16))
            h_ref[:, g * gw:(g + 1) * gw] = hg * e_last[:, g * gw:(g + 1) * gw] + st
        return carry

    lax.fori_loop(0, tb // CHUNK, chunk, 0)

    y = (yb_ref[...] + dsk_ref[...] * xc_ref[:, 0:D_MODEL]) * _silu(z_ref[0])
    outs = []
    for g in range(SSM_GROUPS):
        yg = y[:, g * gw:(g + 1) * gw]
        ms = jnp.mean(yg * yg, axis=-1, keepdims=True)
        outs.append(yg * lax.rsqrt(ms + LN_EPS) * nw_ref[:, g * gw:(g + 1) * gw])
    y_ref[0] = jnp.concatenate(outs, axis=1).astype(y_ref.dtype)


def _ssd(z, xbc, dtx, cw, cb, dtb, alog, dsk, nw, pad):
    b, lp, _ = z.shape
    tb = TIME_BLOCK
    blk = lambda w: pl.BlockSpec((1, tb, w), lambda b_, j: (b_, j, 0))
    par = lambda a: pl.BlockSpec(a.shape, lambda b_, j: (0, 0))
    return pl.pallas_call(
        functools.partial(_ssd_body, pad),
        grid=(b, lp // tb),
        in_specs=[blk(D_MODEL), blk(SSM_CONV_DIM), blk(D_MODEL)] + [par(a) for a in (cw, cb, dtb, alog, dsk, nw)],
        out_specs=blk(D_MODEL),
        out_shape=jax.ShapeDtypeStruct((b, lp, D_MODEL), BF16),
        scratch_shapes=[pltpu.VMEM((tb + HALO, SSM_CONV_DIM), F32),
                        pltpu.VMEM((tb, SSM_CONV_DIM), F32),
                        pltpu.VMEM((tb, D_MODEL), F32),
                        pltpu.VMEM((tb, D_MODEL), F32),
                        pltpu.VMEM((SSM_STATE, D_MODEL), F32)],
        compiler_params=_cparams("parallel", "arbitrary"),
        name="ssd_branch",
    )(z, xbc, dtx, cw, cb, dtb, alog, dsk, nw)


def _head_sum(x, ones_bd):
    outs = []
    for i in range(D_MODEL // LANE):
        xi = x[:, i * LANE:(i + 1) * LANE]
        hi = xi.astype(BF16)
        lo = (xi - hi.astype(F32)).astype(BF16)
        outs.append(_dot(hi, ones_bd) + _dot(lo, ones_bd))
    return jnp.concatenate(outs, axis=1)


def _rwkv_body(cols_ref, mu_ref, w0_ref, wup_ref, a0_ref, aup_ref, gup_ref, kk_ref, ka_ref, rk_ref,
               lng_ref, lnb_ref, y_ref, ext_ref, r_s, k_s, v_s, kk_s, akk_s, lw_s, o_s, s_ref):
    tb = cols_ref.shape[1]
    j = pl.program_id(1)

    @pl.when(j == 0)
    def _():
        s_ref[...] = jnp.zeros(s_ref.shape, F32)

    _shift_halo(ext_ref, cols_ref[0], j == 0)
    cur = ext_ref[HALO:HALO + tb, :]
    prev = ext_ref[pl.ds(HALO - 1, tb), :]
    mixed = cur + (prev - cur) * mu_ref[...]

    lane2 = lax.broadcasted_iota(jnp.int32, (LANE, LANE), 1) // HEAD_DIM
    row2 = lax.broadcasted_iota(jnp.int32, (LANE, LANE), 0) // HEAD_DIM
    ones_bd = (lane2 == row2).astype(BF16)

    r = mixed[:, 0:D_MODEL]
    k = mixed[:, D_MODEL:2 * D_MODEL]
    v = mixed[:, 2 * D_MODEL:3 * D_MODEL]
    lora_wa = mixed[:, 3 * D_MODEL:3 * D_MODEL + LANE]
    lora_g = mixed[:, 3 * D_MODEL + LANE:3 * D_MODEL + 2 * LANE]
    w = -_softplus(-(w0_ref[...] + _dot(jnp.tanh(lora_wa).astype(BF16), wup_ref[...]))) - 0.5
    a = _sigmoid(a0_ref[...] + _dot(lora_wa.astype(BF16), aup_ref[...]))
    gate = _dot(_sigmoid(lora_g).astype(BF16), gup_ref[...])
    kk = k * kk_ref[...]
    kk = kk * lax.rsqrt(jnp.maximum(_head_sum(kk * kk, ones_bd), 1e-24))
    kmod = k * (1.0 + (a - 1.0) * ka_ref[...])
    r_s[...] = r
    k_s[...] = kmod
    v_s[...] = v
    kk_s[...] = kk
    akk_s[...] = a * kk
    lw_s[...] = -jnp.exp(w)

    row, col, bd_mask, tri = _chunk_consts()
    eye = row == col
    strict = row > col
    incl = row >= col
    lvl_masks = []
    s = 2
    while s < CHUNK:
        lvl_masks.append((row // (2 * s) == col // (2 * s)) & ((row // s) % 2 == 1) & ((col // s) % 2 == 0))
        s *= 2
    pair = (row == col + 1) & (row % 2 == 1)

    def bdmm(x, y):
        return _dot(x.astype(BF16), _block_diag(y.astype(BF16), bd_mask))

    def chunk(c, carry):
        rows = pl.ds(pl.multiple_of(c * CHUNK, CHUNK), CHUNK)
        lw = lw_s[rows, :]
        lc = _dot_exact(tri, lw)
        e_p = jnp.exp(lc)
        e_m = jnp.exp(-lc)
        kkc = kk_s[rows, :]
        at = jnp.exp(lc - lw) * kkc
        bt = -(akk_s[rows, :] * e_m)
        kt = k_s[rows, :] * e_m
        rt = r_s[rows, :] * e_p
        p_last = e_p[CHUNK - 1:CHUNK, :]
        bp = bt * p_last
        kp = kt * p_last
        vc = v_s[rows, :]
        for g in range(D_MODEL // GROUP_W):
            sl = slice(g * GROUP_W, (g + 1) * GROUP_W)
            at_g = at[:, sl].astype(BF16)
            rt_g = rt[:, sl].astype(BF16)
            v_g = vc[:, sl].astype(BF16)
            lhs = jnp.concatenate([at_g, rt_g], axis=0)
            g_b = _dot_nt(lhs, _block_diag(bt[:, sl].astype(BF16), bd_mask))
            g_k = _dot_nt(lhs, _block_diag(kt[:, sl].astype(BF16), bd_mask))
            a_ab = jnp.where(strict, g_b[0:CHUNK], 0.0)
            a_rb = jnp.where(incl, g_b[CHUNK:], 0.0)
            a_ak = jnp.where(strict, g_k[0:CHUNK], 0.0)
            a_rk = jnp.where(incl, g_k[CHUNK:], 0.0)
            inv = jnp.where(eye, 1.0, 0.0) + jnp.where(pair, a_ab, 0.0)
            for m in lvl_masks:
                inv = inv + bdmm(bdmm(inv, jnp.where(m, a_ab, 0.0)), inv)
            s_g = s_ref[g].astype(BF16)
            wmat = _dot_nt(at_g, s_g) + bdmm(a_ak, v_g)
            u = bdmm(inv, wmat)
            o_s[rows, sl] = _dot_nt(rt_g, s_g) + bdmm(a_rb, u) + bdmm(a_rk, v_g)
            upd = _dot_tn(jnp.concatenate([u.astype(BF16), v_g], axis=0),
                          jnp.concatenate([bp[:, sl].astype(BF16), kp[:, sl].astype(BF16)], axis=0))
            s_ref[g] = s_ref[g] * p_last[:, sl] + jnp.where(bd_mask, upd, 0.0)
        return carry

    lax.fori_loop(0, tb // CHUNK, chunk, 0)

    o = o_s[...]
    mean = _head_sum(o, ones_bd) * (1.0 / HEAD_DIM)
    oc = o - mean
    var = _head_sum(oc * oc, ones_bd) * (1.0 / HEAD_DIM)
    o = oc * lax.rsqrt(var + RWKV_LN_EPS) * lng_ref[...] + lnb_ref[...]
    rr = r_s[...]
    bonus = _head_sum(rr * k_s[...] * rk_ref[...], ones_bd)
    o = o + bonus * v_s[...]
    y_ref[0] = (o * gate).astype(y_ref.dtype)


def _rwkv(cols, mu, w0, wup, a0, aup, gup, kk, ka, rk, lng, lnb):
    b, lp, _ = cols.shape
    tb = TIME_BLOCK
    par = lambda a: pl.BlockSpec(a.shape, lambda b_, j: (0, 0))
    params = (mu, w0, wup, a0, aup, gup, kk, ka, rk, lng, lnb)
    return pl.pallas_call(
        _rwkv_body,
        grid=(b, lp // tb),
        in_specs=[pl.BlockSpec((1, tb, RWKV_COLS), lambda b_, j: (b_, j, 0))] + [par(a) for a in params],
        out_specs=pl.BlockSpec((1, tb, D_MODEL), lambda b_, j: (b_, j, 0)),
        out_shape=jax.ShapeDtypeStruct((b, lp, D_MODEL), BF16),
        scratch_shapes=[pltpu.VMEM((tb + HALO, RWKV_COLS), F32)]
        + [pltpu.VMEM((tb, D_MODEL), F32) for _ in range(7)]
        + [pltpu.VMEM((D_MODEL // GROUP_W, GROUP_W, GROUP_W), F32)],
        compiler_params=_cparams("parallel", "arbitrary"),
        name="rwkv7_branch",
    )(cols, *params)


def _gelu_tanh(x):
    return 0.5 * x * (1.0 + jnp.tanh(math.sqrt(2.0 / math.pi) * (x + 0.044715 * (x * x * x))))


def _rglru_body(pad, gb_ref, xr_ref, cw_ref, cb_ref, gxw_ref, gxb_ref, gaw_ref, gab_ref, lam_ref,
                y_ref, ext_ref, hc_ref):
    tb = xr_ref.shape[1]
    j = pl.program_id(1)

    @pl.when(j == 0)
    def _():
        hc_ref[...] = jnp.zeros(hc_ref.shape, F32)

    _shift_halo(ext_ref, xr_ref[0], j == 0)
    xf = jnp.broadcast_to(cb_ref[...], (tb, D_MODEL))
    for k in range(CONV_K):
        xf = xf + cw_ref[k:k + 1, :] * ext_ref[pl.ds(HALO - (CONV_K - 1) + k, tb), :]

    gx, ga = [], []
    for hblk in range(LRU_BLOCKS):
        xb = xf[:, hblk * LRU_BLOCK:(hblk + 1) * LRU_BLOCK].astype(BF16)
        gx.append(_dot(xb, gxw_ref[hblk]))
        ga.append(_dot(xb, gaw_ref[hblk]))
    gate_x = _sigmoid(jnp.concatenate(gx, axis=1) + gxb_ref[...])
    gate_a = _sigmoid(jnp.concatenate(ga, axis=1) + gab_ref[...])
    log_a = -LRU_C * gate_a * _softplus(-lam_ref[...])
    a = jnp.exp(log_a)
    u = jnp.sqrt(1.0 - jnp.exp(2.0 * log_a)) * (gate_x * xf)
    row = lax.broadcasted_iota(jnp.int32, (tb, 1), 0)
    u = jnp.where(j * tb + row >= pad, u, 0.0)

    d = 1
    while d < tb:
        keep = row >= d
        a_sh = jnp.where(keep, pltpu.roll(a, d, 0), 1.0)
        u_sh = jnp.where(keep, pltpu.roll(u, d, 0), 0.0)
        u = u + a * u_sh
        a = a * a_sh
        d *= 2
    h = a * hc_ref[...] + u
    hc_ref[...] = h[tb - 1:tb, :]
    y_ref[0] = (h * _gelu_tanh(gb_ref[0])).astype(y_ref.dtype)


def _rglru(proj, cw, cb, gxw, gxb, gaw, gab, lam, pad):
    b, lp, _ = proj.shape
    tb = TIME_BLOCK
    par2 = lambda a: pl.BlockSpec(a.shape, lambda b_, j: (0, 0))
    par3 = lambda a: pl.BlockSpec(a.shape, lambda b_, j: (0, 0, 0))
    return pl.pallas_call(
        functools.partial(_rglru_body, pad),
        grid=(b, lp // tb),
        in_specs=[pl.BlockSpec((1, tb, D_MODEL), lambda b_, j: (b_, j, 0)),
                  pl.BlockSpec((1, tb, D_MODEL), lambda b_, j: (b_, j, 1)),
                  par2(cw), par2(cb), par3(gxw), par2(gxb), par3(gaw), par2(gab), par2(lam)],
        out_specs=pl.BlockSpec((1, tb, D_MODEL), lambda b_, j: (b_, j, 0)),
        out_shape=jax.ShapeDtypeStruct((b, lp, D_MODEL), BF16),
        scratch_shapes=[pltpu.VMEM((tb + HALO, D_MODEL), F32), pltpu.VMEM((1, D_MODEL), F32)],
        compiler_params=_cparams("parallel", "arbitrary"),
        name="rglru_branch",
    )(proj, proj, cw, cb, gxw, gxb, gaw, gab, lam)


def _proj_ln_body(n_in, with_router, *refs):
    xs = refs[:n_in]
    ws = refs[n_in:2 * n_in]
    h_ref, g_ref, b_ref = refs[2 * n_in:2 * n_in + 3]
    rest = refs[2 * n_in + 3:]
    mix = _dot(xs[0][0], ws[0][...])
    for x_ref, w_ref in zip(xs[1:], ws[1:]):
        mix = mix + _dot(x_ref[0], w_ref[...])
    out = _layer_norm(DEEPNORM_ALPHA * h_ref[0] + mix, g_ref[...], b_ref[...])
    if not with_router:
        o_ref, ob_ref = rest
    else:
        wr_ref, o_ref, ob_ref, gate_ref = rest
        logits = _dot_exact(out, wr_ref[...])
        lane = lax.broadcasted_iota(jnp.int32, logits.shape, 1)
        logits = jnp.where(lane < N_EXPERTS, logits, -jnp.inf)
        m1 = jnp.max(logits, axis=-1, keepdims=True)
        i1 = jnp.min(jnp.where(logits == m1, lane, LANE), axis=-1, keepdims=True)
        rest_l = jnp.where(lane == i1, -jnp.inf, logits)
        m2 = jnp.max(rest_l, axis=-1, keepdims=True)
        i2 = jnp.min(jnp.where(rest_l == m2, lane, LANE), axis=-1, keepdims=True)
        e2 = jnp.exp(m2 - m1)
        g1 = 1.0 / (1.0 + e2)
        g2 = e2 / (1.0 + e2)
        gate_ref[0] = jnp.where(lane == i1, g1, 0.0) + jnp.where(lane == i2, g2, 0.0)
    o_ref[0] = out
    ob_ref[0] = out.astype(BF16)


def _proj_ln(xs, ws, h, g, bb, tm, w_router=None):
    b, lp, _ = h.shape
    n_in = len(xs)
    row = lambda w: pl.BlockSpec((1, tm, w), lambda b_, i: (b_, i, 0))
    par = lambda a: pl.BlockSpec(a.shape, lambda b_, i: (0, 0))
    in_specs = [row(x.shape[2]) for x in xs] + [par(w) for w in ws] + [row(D_MODEL), par(g), par(bb)]
    out_specs = [row(D_MODEL), row(D_MODEL)]
    out_shape = [jax.ShapeDtypeStruct((b, lp, D_MODEL), F32), jax.ShapeDtypeStruct((b, lp, D_MODEL), BF16)]
    args = list(xs) + list(ws) + [h, g, bb]
    if w_router is not None:
        in_specs.append(par(w_router))
        args.append(w_router)
        out_specs.append(row(LANE))
        out_shape.append(jax.ShapeDtypeStruct((b, lp, LANE), F32))
    return pl.pallas_call(
        functools.partial(_proj_ln_body, n_in, w_router is not None),
        grid=(b, lp // tm),
        in_specs=in_specs, out_specs=out_specs, out_shape=out_shape,
        compiler_params=_cparams("parallel", "parallel"),
        name="out_proj_layernorm",
    )(*args)


def _ffn_body(pad, n_exp, xb_ref, h_ref, wg_ref, wu_ref, wd_ref, g_ref, b_ref, *rest):
    if n_exp:
        gate_ref, o_ref, acc_ref = rest
        e = pl.program_id(2)
        f = pl.program_id(3)
        first = (e == 0) & (f == 0)
        last = (e == n_exp - 1) & (f == pl.num_programs(3) - 1)
    else:
        o_ref, acc_ref = rest
        f = pl.program_id(2)
        first = f == 0
        last = f == pl.num_programs(2) - 1
    tm = xb_ref.shape[1]

    @pl.when(first)
    def _():
        acc_ref[...] = jnp.zeros(acc_ref.shape, F32)

    x = xb_ref[0]
    wg = wg_ref[0] if n_exp else wg_ref[...]
    wu = wu_ref[0] if n_exp else wu_ref[...]
    wd = wd_ref[0] if n_exp else wd_ref[...]
    act = _silu(_dot(x, wg)) * _dot(x, wu)
    if n_exp:
        lane = lax.broadcasted_iota(jnp.int32, (tm, LANE), 1)
        act = act * jnp.sum(jnp.where(lane == e, gate_ref[0], 0.0), axis=-1, keepdims=True)
    acc_ref[...] += _dot(act.astype(BF16), wd)

    @pl.when(last)
    def _():
        out = _layer_norm(DEEPNORM_ALPHA * h_ref[0] + acc_ref[...], g_ref[...], b_ref[...])
        if pad:
            pos = pl.program_id(1) * tm + lax.broadcasted_iota(jnp.int32, (tm, 1), 0)
            out = jnp.where(pos >= pad, out, 0.0)
        o_ref[0] = out.astype(o_ref.dtype)


def _ffn(xb, h, w_gu, w_down, g, bb, tm, out_dtype, zero_pad):
    b, lp, _ = h.shape
    nf = D_FF // FF_TILE
    row = pl.BlockSpec((1, tm, D_MODEL), lambda b_, i, f: (b_, i, 0))
    par = lambda a: pl.BlockSpec(a.shape, lambda b_, i, f: (0, 0))
    return pl.pallas_call(
        functools.partial(_ffn_body, zero_pad, 0),
        grid=(b, lp // tm, nf),
        in_specs=[row, row,
                  pl.BlockSpec((D_MODEL, FF_TILE), lambda b_, i, f: (0, f)),
                  pl.BlockSpec((D_MODEL, FF_TILE), lambda b_, i, f: (0, f + nf)),
                  pl.BlockSpec((FF_TILE, D_MODEL), lambda b_, i, f: (f, 0)),
                  par(g), par(bb)],
        out_specs=row,
        out_shape=jax.ShapeDtypeStruct((b, lp, D_MODEL), out_dtype),
        scratch_shapes=[pltpu.VMEM((tm, D_MODEL), F32)],
        compiler_params=_cparams("parallel", "parallel", "arbitrary"),
        name="swiglu_layernorm",
    )(xb, h, w_gu, w_gu, w_down, g, bb)


def _moe(xb, h, gates, w_gu, w_down, g, bb, tm):
    b, lp, _ = h.shape
    nf = D_FF // FF_TILE
    row = lambda w: pl.BlockSpec((1, tm, w), lambda b_, i, e, f: (b_, i, 0))
    par = lambda a: pl.BlockSpec(a.shape, lambda b_, i, e, f: (0, 0))
    return pl.pallas_call(
        functools.partial(_ffn_body, 0, N_EXPERTS),
        grid=(b, lp // tm, N_EXPERTS, nf),
        in_specs=[row(D_MODEL), row(D_MODEL),
                  pl.BlockSpec((1, D_MODEL, FF_TILE), lambda b_, i, e, f: (e, 0, f)),
                  pl.BlockSpec((1, D_MODEL, FF_TILE), lambda b_, i, e, f: (e, 0, f + nf)),
                  pl.BlockSpec((1, FF_TILE, D_MODEL), lambda b_, i, e, f: (e, f, 0)),
                  par(g), par(bb), row(LANE)],
        out_specs=row(D_MODEL),
        out_shape=jax.ShapeDtypeStruct((b, lp, D_MODEL), F32),
        scratch_shapes=[pltpu.VMEM((tm, D_MODEL), F32)],
        compiler_params=_cparams("parallel", "parallel", "arbitrary", "arbitrary"),
        name="moe_swiglu_layernorm",
    )(xb, h, w_gu, w_gu, w_down, g, bb, gates)


def _rep(v):
    return jnp.repeat(v.astype(F32), HEAD_DIM)[None, :]


def _row(v):
    return v.astype(F32).reshape(1, -1)


def kernel(x, meta, ev_w_in, ev_conv_w, ev_conv_b, ev_dt_bias, ev_a_log, ev_d_skip, ev_ssm_norm, ev_shift_mu, ev_w0, ev_w_up, ev_a0, ev_a_up, ev_g_up, ev_k_k, ev_k_a, ev_r_k, ev_lnx_g, ev_lnx_b, ev_w_out, ev_ln1_g, ev_ln1_b, ev_ffn_w_gu, ev_ffn_w_down, ev_ln2_g, ev_ln2_b, od_w_in, od_conv_w, od_conv_b, od_gx_w, od_gx_b, od_ga_w, od_ga_b, od_lambda, od_w_out, od_ln1_g, od_ln1_b, od_router, od_exp_w_gu, od_exp_w_down, od_ln2_g, od_ln2_b):
    b, seq, d = x.shape
    assert d == D_MODEL
    l = seq + N_META
    pad = (-l) % CHUNK
    lp = l + pad
    assert lp % TIME_BLOCK == 0, lp
    tm = _row_tile(lp)

    h = jnp.concatenate([jnp.zeros((b, pad, d), x.dtype),
                         jnp.broadcast_to(meta.astype(x.dtype)[None], (b, N_META, d)), x], axis=1)
    hb = h.astype(BF16)

    i = 0
    w_in = ev_w_in[i]
    o1 = D_MODEL
    o2 = o1 + SSM_CONV_DIM
    o3 = o2 + N_HEADS
    z = _mm(hb, w_in[:, :o1].astype(BF16), F32, tm)
    xbc = _mm(hb, w_in[:, o1:o2].astype(BF16), F32, tm)
    dtx = _mm(hb, jnp.repeat(w_in[:, o2:o3], HEAD_DIM, axis=1).astype(BF16), F32, tm)
    cols = _mm(hb, w_in[:, o3:].astype(BF16), F32, tm)

    y_a = _ssd(z, xbc, dtx, ev_conv_w[i], _row(ev_conv_b[i]), _rep(ev_dt_bias[i]), _rep(ev_a_log[i]),
               _rep(ev_d_skip[i]), _row(ev_ssm_norm[i]), pad)

    zeros64 = jnp.zeros((64, D_MODEL), F32)
    wup = jnp.concatenate([ev_w_up[i], zeros64], axis=0).astype(BF16)
    aup = jnp.concatenate([zeros64, ev_a_up[i]], axis=0).astype(BF16)
    y_b = _rwkv(cols, _row(ev_shift_mu[i]), _row(ev_w0[i]), wup, _row(ev_a0[i]), aup, ev_g_up[i].astype(BF16),
                _row(ev_k_k[i]), _row(ev_k_a[i]), _row(ev_r_k[i]), _row(ev_lnx_g[i]), _row(ev_lnx_b[i]))

    w_out = ev_w_out[i].astype(BF16)
    h, hb = _proj_ln([y_a, y_b], [w_out[:D_MODEL], w_out[D_MODEL:]], h, _row(ev_ln1_g[i]), _row(ev_ln1_b[i]), tm)
    h = _ffn(hb, h, ev_ffn_w_gu[i].astype(BF16), ev_ffn_w_down[i].astype(BF16),
             _row(ev_ln2_g[i]), _row(ev_ln2_b[i]), tm, F32, pad)
    hb = h.astype(BF16)

    proj = _mm(hb, od_w_in[i].astype(BF16), F32, tm)
    y_c = _rglru(proj, od_conv_w[i], _row(od_conv_b[i]), od_gx_w[i].astype(BF16), _row(od_gx_b[i]),
                 od_ga_w[i].astype(BF16), _row(od_ga_b[i]), _row(od_lambda[i]), pad)
    w_router = jnp.pad(od_router[i], ((0, 0), (0, LANE - N_EXPERTS)))
    h, hb, gates = _proj_ln([y_c], [od_w_out[i].astype(BF16)], h, _row(od_ln1_g[i]), _row(od_ln1_b[i]), tm,
                            w_router=w_router)
    h = _moe(hb, h, gates, od_exp_w_gu[i].astype(BF16), od_exp_w_down[i].astype(BF16),
             _row(od_ln2_g[i]), _row(od_ln2_b[i]), tm)
    return h[:, pad + N_META:]
```

```python
import functools
import math

import jax
import jax.numpy as jnp
from jax import lax
from jax.experimental import pallas as pl
from jax.experimental.pallas import tpu as pltpu

F32 = jnp.float32
BF16 = jnp.bfloat16
HIGHEST = lax.Precision.HIGHEST

D_MODEL = 1024
N_META = 16
CHUNK = 64
HEAD_DIM = 64
N_HEADS = D_MODEL // HEAD_DIM
SSM_GROUPS = 2
SSM_STATE = 128
SSM_CONV_DIM = D_MODEL + 2 * SSM_GROUPS * SSM_STATE
CONV_K = 4
RWKV_LORA = 256
RWKV_COLS = 3 * D_MODEL + RWKV_LORA
RWKV_LN_EPS = 64e-5
LRU_BLOCKS = 8
LRU_BLOCK = D_MODEL // LRU_BLOCKS
LRU_C = 8.0
D_FF = 2816
N_EXPERTS = 8
DEPTH = 2
DEEPNORM_ALPHA = (2 * DEPTH) ** 0.25
LN_EPS = 1e-5

LANE = 128
HALO = 8
GROUP_HEADS = 4
GROUP_W = GROUP_HEADS * HEAD_DIM
TIME_BLOCK = 192
FF_TILE = 256
EXPERT_TILE = 512
VMEM_LIMIT = 56 * 1024 * 1024


def _cparams(*sem):
    return pltpu.CompilerParams(dimension_semantics=sem, vmem_limit_bytes=VMEM_LIMIT)


def _row_tile(lp):
    best = 16
    for t in range(16, min(lp, 1376) + 1, 16):
        if lp % t == 0:
            best = t
    return best


def _col_tile(n):
    for t in (512, 1664, 256, 128):
        if n % t == 0:
            return t
    raise ValueError(n)


def _sigmoid(x):
    return 1.0 / (1.0 + jnp.exp(-x))


def _silu(x):
    return x * _sigmoid(x)


def _softplus(x):
    return jnp.maximum(x, 0.0) + jnp.log(1.0 + jnp.exp(-jnp.abs(x)))


def _dot(a, b):
    return jnp.dot(a, b, preferred_element_type=F32)


def _dot_nt(a, b):
    return lax.dot_general(a, b, (((1,), (1,)), ((), ())), preferred_element_type=F32)


def _dot_tn(a, b):
    return lax.dot_general(a, b, (((0,), (0,)), ((), ())), preferred_element_type=F32)


def _dot_exact(a, b):
    return jnp.dot(a, b, preferred_element_type=F32, precision=HIGHEST)


def _layer_norm(x, g, b):
    mu = jnp.mean(x, axis=-1, keepdims=True)
    xc = x - mu
    var = jnp.mean(xc * xc, axis=-1, keepdims=True)
    return xc * lax.rsqrt(var + LN_EPS) * g + b


def _mm_body(x_ref, w_ref, o_ref):
    o_ref[0] = _dot(x_ref[0], w_ref[...]).astype(o_ref.dtype)


def _mm(x, w, out_dtype, tm):
    b, lp, k = x.shape
    n = w.shape[1]
    tn = _col_tile(n)
    return pl.pallas_call(
        _mm_body,
        grid=(b, lp // tm, n // tn),
        in_specs=[pl.BlockSpec((1, tm, k), lambda b_, i, j: (b_, i, 0)),
                  pl.BlockSpec((k, tn), lambda b_, i, j: (0, j))],
        out_specs=pl.BlockSpec((1, tm, tn), lambda b_, i, j: (b_, i, j)),
        out_shape=jax.ShapeDtypeStruct((b, lp, n), out_dtype),
        compiler_params=_cparams("parallel", "parallel", "arbitrary"),
        name="dense_matmul",
    )(x, w)


def _chunk_consts():
    row = lax.broadcasted_iota(jnp.int32, (CHUNK, GROUP_W), 0)
    col = lax.broadcasted_iota(jnp.int32, (CHUNK, GROUP_W), 1) % HEAD_DIM
    r2 = lax.broadcasted_iota(jnp.int32, (GROUP_W, GROUP_W), 0) // HEAD_DIM
    c2 = lax.broadcasted_iota(jnp.int32, (GROUP_W, GROUP_W), 1) // HEAD_DIM
    tri_r = lax.broadcasted_iota(jnp.int32, (CHUNK, CHUNK), 0)
    tri_c = lax.broadcasted_iota(jnp.int32, (CHUNK, CHUNK), 1)
    tri = (tri_c <= tri_r).astype(F32)
    return row, col, (r2 == c2), tri


def _block_diag(y, bd_mask):
    return jnp.where(bd_mask, jnp.concatenate([y] * GROUP_HEADS, axis=0), jnp.zeros((), y.dtype))


def _shift_halo(ext_ref, blk, first):
    tb = blk.shape[0]

    @pl.when(first)
    def _():
        ext_ref[0:HALO, :] = jnp.zeros((HALO, ext_ref.shape[1]), F32)

    @pl.when(jnp.logical_not(first))
    def _():
        ext_ref[0:HALO, :] = ext_ref[tb:tb + HALO, :]

    ext_ref[HALO:HALO + tb, :] = blk


def _ssd_body(pad, z_ref, xbc_ref, dtx_ref, cw_ref, cb_ref, dtb_ref, alog_ref, dsk_ref, nw_ref,
              y_ref, ext_ref, xc_ref, dt_ref, yb_ref, h_ref):
    tb = xbc_ref.shape[1]
    j = pl.program_id(1)

    @pl.when(j == 0)
    def _():
        h_ref[...] = jnp.zeros(h_ref.shape, F32)

    _shift_halo(ext_ref, xbc_ref[0], j == 0)
    acc = jnp.broadcast_to(cb_ref[...], (tb, SSM_CONV_DIM))
    for k in range(CONV_K):
        acc = acc + cw_ref[k:k + 1, :] * ext_ref[pl.ds(HALO - (CONV_K - 1) + k, tb), :]
    xc_ref[...] = _silu(acc)

    pos = j * tb + lax.broadcasted_iota(jnp.int32, (tb, 1), 0)
    dt_ref[...] = jnp.where(pos >= pad, _softplus(dtx_ref[0] + dtb_ref[...]), 0.0)

    row, col, bd_mask, tri = _chunk_consts()
    row_w = jnp.concatenate([row] * (D_MODEL // GROUP_W), axis=1)
    col_w = jnp.concatenate([col] * (D_MODEL // GROUP_W), axis=1)
    a_neg = -jnp.exp(alog_ref[...])
    gw = D_MODEL // SSM_GROUPS

    def chunk(c, carry):
        rows = pl.ds(pl.multiple_of(c * CHUNK, CHUNK), CHUNK)
        xs = xc_ref[rows, 0:D_MODEL]
        dt = dt_ref[rows, :]
        xdt = xs * dt
        acs = _dot_exact(tri, dt * a_neg)
        acs_t = jnp.sum(jnp.where(row_w == col_w, acs, 0.0), axis=0, keepdims=True)
        decay = jnp.where(row_w >= col_w, jnp.exp(jnp.minimum(acs - acs_t, 0.0)), 0.0)
        e_acs = jnp.exp(acs)
        last = acs[CHUNK - 1:CHUNK, :]
        xend = xdt * jnp.exp(last - acs)
        e_last = jnp.exp(last)
        for g in range(SSM_GROUPS):
            bg = xc_ref[rows, D_MODEL + g * SSM_STATE:D_MODEL + (g + 1) * SSM_STATE].astype(BF16)
            cg = xc_ref[rows, D_MODEL + (SSM_GROUPS + g) * SSM_STATE:
                        D_MODEL + (SSM_GROUPS + g + 1) * SSM_STATE].astype(BF16)
            scores = _dot_nt(cg, jnp.concatenate([bg] * GROUP_HEADS, axis=0))
            hg = h_ref[:, g * gw:(g + 1) * gw]
            y_off = _dot(cg, hg.astype(BF16)) * e_acs[:, g * gw:(g + 1) * gw]
            for t in range(gw // GROUP_W):
                lo = g * gw + t * GROUP_W
                m = (scores * decay[:, lo:lo + GROUP_W]).astype(BF16)
                xbd = _block_diag(xdt[:, lo:lo + GROUP_W].astype(BF16), bd_mask)
                yb_ref[rows, lo:lo + GROUP_W] = (_dot(m, xbd) + y_off[:, t * GROUP_W:(t + 1) * GROUP_W])
            st = _dot_tn(bg, xend[:, g * gw:(g + 1) * gw].astype(BF16))
            h_ref[:, g * gw:(g + 1) * gw] = hg * e_last[:, g * gw:(g + 1) * gw] + st
        return carry

    lax.fori_loop(0, tb // CHUNK, chunk, 0)

    y = (yb_ref[...] + dsk_ref[...] * xc_ref[:, 0:D_MODEL]) * _silu(z_ref[0])
    outs = []
    for g in range(SSM_GROUPS):
        yg = y[:, g * gw:(g + 1) * gw]
        ms = jnp.mean(yg * yg, axis=-1, keepdims=True)
        outs.append(yg * lax.rsqrt(ms + LN_EPS) * nw_ref[:, g * gw:(g + 1) * gw])
    y_ref[0] = jnp.concatenate(outs, axis=1).astype(y_ref.dtype)


def _ssd(z, xbc, dtx, cw, cb, dtb, alog, dsk, nw, pad):
    b, lp, _ = z.shape
    tb = TIME_BLOCK
    blk = lambda w: pl.BlockSpec((1, tb, w), lambda b_, j: (b_, j, 0))
    par = lambda a: pl.BlockSpec(a.shape, lambda b_, j: (0, 0))
    return pl.pallas_call(
        functools.partial(_ssd_body, pad),
        grid=(b, lp // tb),
        in_specs=[blk(D_MODEL), blk(SSM_CONV_DIM), blk(D_MODEL)] + [par(a) for a in (cw, cb, dtb, alog, dsk, nw)],
        out_specs=blk(D_MODEL),
        out_shape=jax.ShapeDtypeStruct((b, lp, D_MODEL), BF16),
        scratch_shapes=[pltpu.VMEM((tb + HALO, SSM_CONV_DIM), F32),
                        pltpu.VMEM((tb, SSM_CONV_DIM), F32),
                        pltpu.VMEM((tb, D_MODEL), F32),
                        pltpu.VMEM((tb, D_MODEL), F32),
                        pltpu.VMEM((SSM_STATE, D_MODEL), F32)],
        compiler_params=_cparams("parallel", "arbitrary"),
        name="ssd_branch",
    )(z, xbc, dtx, cw, cb, dtb, alog, dsk, nw)


def _head_sum(x, ones_bd):
    outs = []
    for i in range(D_MODEL // LANE):
        xi = x[:, i * LANE:(i + 1) * LANE]
        hi = xi.astype(BF16)
        lo = (xi - hi.astype(F32)).astype(BF16)
        outs.append(_dot(hi, ones_bd) + _dot(lo, ones_bd))
    return jnp.concatenate(outs, axis=1)


def _rwkv_body(cols_ref, mu_ref, w0_ref, wup_ref, a0_ref, aup_ref, gup_ref, kk_ref, ka_ref, rk_ref,
               lng_ref, lnb_ref, y_ref, ext_ref, r_s, k_s, v_s, kk_s, akk_s, lw_s, o_s, s_ref):
    tb = cols_ref.shape[1]
    j = pl.program_id(1)

    @pl.when(j == 0)
    def _():
        s_ref[...] = jnp.zeros(s_ref.shape, F32)

    _shift_halo(ext_ref, cols_ref[0], j == 0)
    cur = ext_ref[HALO:HALO + tb, :]
    prev = ext_ref[pl.ds(HALO - 1, tb), :]
    mixed = cur + (prev - cur) * mu_ref[...]

    lane2 = lax.broadcasted_iota(jnp.int32, (LANE, LANE), 1) // HEAD_DIM
    row2 = lax.broadcasted_iota(jnp.int32, (LANE, LANE), 0) // HEAD_DIM
    ones_bd = (lane2 == row2).astype(BF16)

    r = mixed[:, 0:D_MODEL]
    k = mixed[:, D_MODEL:2 * D_MODEL]
    v = mixed[:, 2 * D_MODEL:3 * D_MODEL]
    lora_wa = mixed[:, 3 * D_MODEL:3 * D_MODEL + LANE]
    lora_g = mixed[:, 3 * D_MODEL + LANE:3 * D_MODEL + 2 * LANE]
    w = -_softplus(-(w0_ref[...] + _dot(jnp.tanh(lora_wa).astype(BF16), wup_ref[...]))) - 0.5
    a = _sigmoid(a0_ref[...] + _dot(lora_wa.astype(BF16), aup_ref[...]))
    gate = _dot(_sigmoid(lora_g).astype(BF16), gup_ref[...])
    kk = k * kk_ref[...]
    kk = kk * lax.rsqrt(jnp.maximum(_head_sum(kk * kk, ones_bd), 1e-24))
    kmod = k * (1.0 + (a - 1.0) * ka_ref[...])
    r_s[...] = r
    k_s[...] = kmod
    v_s[...] = v
    kk_s[...] = kk
    akk_s[...] = a * kk
    lw_s[...] = -jnp.exp(w)

    row, col, bd_mask, tri = _chunk_consts()
    eye = row == col
    strict = row > col
    incl = row >= col
    lvl_masks = []
    s = 2
    while s < CHUNK:
        lvl_masks.append((row // (2 * s) == col // (2 * s)) & ((row // s) % 2 == 1) & ((col // s) % 2 == 0))
        s *= 2
    pair = (row == col + 1) & (row % 2 == 1)

    def bdmm(x, y):
        return _dot(x.astype(BF16), _block_diag(y.astype(BF16), bd_mask))

    def chunk(c, carry):
        rows = pl.ds(pl.multiple_of(c * CHUNK, CHUNK), CHUNK)
        lw = lw_s[rows, :]
        lc = _dot_exact(tri, lw)
        e_p = jnp.exp(lc)
        e_m = jnp.exp(-lc)
        kkc = kk_s[rows, :]
        at = jnp.exp(lc - lw) * kkc
        bt = -(akk_s[rows, :] * e_m)
        kt = k_s[rows, :] * e_m
        rt = r_s[rows, :] * e_p
        p_last = e_p[CHUNK - 1:CHUNK, :]
        bp = bt * p_last
        kp = kt * p_last
        vc = v_s[rows, :]
        for g in range(D_MODEL // GROUP_W):
            sl = slice(g * GROUP_W, (g + 1) * GROUP_W)
            at_g = at[:, sl].astype(BF16)
            rt_g = rt[:, sl].astype(BF16)
            v_g = vc[:, sl].astype(BF16)
            lhs = jnp.concatenate([at_g, rt_g], axis=0)
            g_b = _dot_nt(lhs, _block_diag(bt[:, sl].astype(BF16), bd_mask))
            g_k = _dot_nt(lhs, _block_diag(kt[:, sl].astype(BF16), bd_mask))
            a_ab = jnp.where(strict, g_b[0:CHUNK], 0.0)
            a_rb = jnp.where(incl, g_b[CHUNK:], 0.0)
            a_ak = jnp.where(strict, g_k[0:CHUNK], 0.0)
            a_rk = jnp.where(incl, g_k[CHUNK:], 0.0)
            inv = jnp.where(eye, 1.0, 0.0) + jnp.where(pair, a_ab, 0.0)
            for m in lvl_masks:
                inv = inv + bdmm(bdmm(inv, jnp.where(m, a_ab, 0.0)), inv)
            s_g = s_ref[g].astype(BF16)
            wmat = _dot_nt(at_g, s_g) + bdmm(a_ak, v_g)
            u = bdmm(inv, wmat)
            o_s[rows, sl] = _dot_nt(rt_g, s_g) + bdmm(a_rb, u) + bdmm(a_rk, v_g)
            upd = _dot_tn(jnp.concatenate([u.astype(BF16), v_g], axis=0),
                          jnp.concatenate([bp[:, sl].astype(BF16), kp[:, sl].astype(BF16)], axis=0))
            s_ref[g] = s_ref[g] * p_last[:, sl] + jnp.where(bd_mask, upd, 0.0)
        return carry

    lax.fori_loop(0, tb // CHUNK, chunk, 0)

    o = o_s[...]
    mean = _head_sum(o, ones_bd) * (1.0 / HEAD_DIM)
    oc = o - mean
    var = _head_sum(oc * oc, ones_bd) * (1.0 / HEAD_DIM)
    o = oc * lax.rsqrt(var + RWKV_LN_EPS) * lng_ref[...] + lnb_ref[...]
    rr = r_s[...]
    bonus = _head_sum(rr * k_s[...] * rk_ref[...], ones_bd)
    o = o + bonus * v_s[...]
    y_ref[0] = (o * gate).astype(y_ref.dtype)


def _rwkv(cols, mu, w0, wup, a0, aup, gup, kk, ka, rk, lng, lnb):
    b, lp, _ = cols.shape
    tb = TIME_BLOCK
    par = lambda a: pl.BlockSpec(a.shape, lambda b_, j: (0, 0))
    params = (mu, w0, wup, a0, aup, gup, kk, ka, rk, lng, lnb)
    return pl.pallas_call(
        _rwkv_body,
        grid=(b, lp // tb),
        in_specs=[pl.BlockSpec((1, tb, RWKV_COLS), lambda b_, j: (b_, j, 0))] + [par(a) for a in params],
        out_specs=pl.BlockSpec((1, tb, D_MODEL), lambda b_, j: (b_, j, 0)),
        out_shape=jax.ShapeDtypeStruct((b, lp, D_MODEL), BF16),
        scratch_shapes=[pltpu.VMEM((tb + HALO, RWKV_COLS), F32)]
        + [pltpu.VMEM((tb, D_MODEL), F32) for _ in range(7)]
        + [pltpu.VMEM((D_MODEL // GROUP_W, GROUP_W, GROUP_W), F32)],
        compiler_params=_cparams("parallel", "arbitrary"),
        name="rwkv7_branch",
    )(cols, *params)


def _gelu_tanh(x):
    return 0.5 * x * (1.0 + jnp.tanh(math.sqrt(2.0 / math.pi) * (x + 0.044715 * (x * x * x))))


def _rglru_body(pad, gb_ref, xr_ref, cw_ref, cb_ref, gxw_ref, gxb_ref, gaw_ref, gab_ref, lam_ref,
                y_ref, ext_ref, hc_ref):
    tb = xr_ref.shape[1]
    j = pl.program_id(1)

    @pl.when(j == 0)
    def _():
        hc_ref[...] = jnp.zeros(hc_ref.shape, F32)

    _shift_halo(ext_ref, xr_ref[0], j == 0)
    xf = jnp.broadcast_to(cb_ref[...], (tb, D_MODEL))
    for k in range(CONV_K):
        xf = xf + cw_ref[k:k + 1, :] * ext_ref[pl.ds(HALO - (CONV_K - 1) + k, tb), :]

    gx, ga = [], []
    for hblk in range(LRU_BLOCKS):
        xb = xf[:, hblk * LRU_BLOCK:(hblk + 1) * LRU_BLOCK].astype(BF16)
        gx.append(_dot(xb, gxw_ref[hblk]))
        ga.append(_dot(xb, gaw_ref[hblk]))
    gate_x = _sigmoid(jnp.concatenate(gx, axis=1) + gxb_ref[...])
    gate_a = _sigmoid(jnp.concatenate(ga, axis=1) + gab_ref[...])
    log_a = -LRU_C * gate_a * _softplus(-lam_ref[...])
    a = jnp.exp(log_a)
    u = jnp.sqrt(1.0 - jnp.exp(2.0 * log_a)) * (gate_x * xf)
    row = lax.broadcasted_iota(jnp.int32, (tb, 1), 0)
    u = jnp.where(j * tb + row >= pad, u, 0.0)

    d = 1
    while d < tb:
        keep = row >= d
        a_sh = jnp.where(keep, pltpu.roll(a, d, 0), 1.0)
        u_sh = jnp.where(keep, pltpu.roll(u, d, 0), 0.0)
        u = u + a * u_sh
        a = a * a_sh
        d *= 2
    h = a * hc_ref[...] + u
    hc_ref[...] = h[tb - 1:tb, :]
    y_ref[0] = (h * _gelu_tanh(gb_ref[0])).astype(y_ref.dtype)


def _rglru(proj, cw, cb, gxw, gxb, gaw, gab, lam, pad):
    b, lp, _ = proj.shape
    tb = TIME_BLOCK
    par2 = lambda a: pl.BlockSpec(a.shape, lambda b_, j: (0, 0))
    par3 = lambda a: pl.BlockSpec(a.shape, lambda b_, j: (0, 0, 0))
    return pl.pallas_call(
        functools.partial(_rglru_body, pad),
        grid=(b, lp // tb),
        in_specs=[pl.BlockSpec((1, tb, D_MODEL), lambda b_, j: (b_, j, 0)),
                  pl.BlockSpec((1, tb, D_MODEL), lambda b_, j: (b_, j, 1)),
                  par2(cw), par2(cb), par3(gxw), par2(gxb), par3(gaw), par2(gab), par2(lam)],
        out_specs=pl.BlockSpec((1, tb, D_MODEL), lambda b_, j: (b_, j, 0)),
        out_shape=jax.ShapeDtypeStruct((b, lp, D_MODEL), BF16),
        scratch_shapes=[pltpu.VMEM((tb + HALO, D_MODEL), F32), pltpu.VMEM((1, D_MODEL), F32)],
        compiler_params=_cparams("parallel", "arbitrary"),
        name="rglru_branch",
    )(proj, proj, cw, cb, gxw, gxb, gaw, gab, lam)


def _proj_ln_body(n_in, with_router, *refs):
    xs = refs[:n_in]
    ws = refs[n_in:2 * n_in]
    h_ref, g_ref, b_ref = refs[2 * n_in:2 * n_in + 3]
    rest = refs[2 * n_in + 3:]
    mix = _dot(xs[0][0], ws[0][...])
    for x_ref, w_ref in zip(xs[1:], ws[1:]):
        mix = mix + _dot(x_ref[0], w_ref[...])
    out = _layer_norm(DEEPNORM_ALPHA * h_ref[0] + mix, g_ref[...], b_ref[...])
    if not with_router:
        o_ref, ob_ref = rest
        ob_ref[0] = out.astype(BF16)
    else:
        wr_ref, o_ref, gate_ref, dest_ref, cnt_ref, base_ref = rest
        tm = out.shape[0]
        first = (pl.program_id(0) == 0) & (pl.program_id(1) == 0)

        @pl.when(first)
        def _():
            base_ref[...] = jnp.zeros(base_ref.shape, F32)

        logits = _dot_exact(out, wr_ref[...])
        lane = lax.broadcasted_iota(jnp.int32, logits.shape, 1)
        logits = jnp.where(lane < N_EXPERTS, logits, -jnp.inf)
        m1 = jnp.max(logits, axis=-1, keepdims=True)
        i1 = jnp.min(jnp.where(logits == m1, lane, LANE), axis=-1, keepdims=True)
        rest_l = jnp.where(lane == i1, -jnp.inf, logits)
        m2 = jnp.max(rest_l, axis=-1, keepdims=True)
        i2 = jnp.min(jnp.where(rest_l == m2, lane, LANE), axis=-1, keepdims=True)
        e2 = jnp.exp(m2 - m1)
        gate_ref[0] = jnp.where(lane == 0, 1.0 / (1.0 + e2), 0.0) + jnp.where(lane == 1, e2 / (1.0 + e2), 0.0)
        hot1 = lane == i1
        hot2 = lane == i2
        both = jnp.where(hot1 | hot2, 1.0, 0.0)
        tri_r = lax.broadcasted_iota(jnp.int32, (tm, tm), 0)
        tri_c = lax.broadcasted_iota(jnp.int32, (tm, tm), 1)
        before = _dot(jnp.where(tri_c < tri_r, 1.0, 0.0).astype(BF16), both.astype(BF16)) + base_ref[...]
        rank1 = jnp.sum(jnp.where(hot1, before, 0.0), axis=-1, keepdims=True).astype(jnp.int32)
        rank2 = jnp.sum(jnp.where(hot2, before, 0.0), axis=-1, keepdims=True).astype(jnp.int32)
        dest_ref[0] = (jnp.where(lane == 0, i1, 0) + jnp.where(lane == 1, i2, 0)
                       + jnp.where(lane == 2, rank1, 0) + jnp.where(lane == 3, rank2, 0))
        base_ref[...] += jnp.sum(both, axis=0, keepdims=True)
        cnt_ref[...] = base_ref[...].astype(jnp.int32)
    o_ref[0] = out


def _proj_ln(xs, ws, h, g, bb, tm, w_router=None):
    b, lp, _ = h.shape
    n_in = len(xs)
    row = lambda w: pl.BlockSpec((1, tm, w), lambda b_, i: (b_, i, 0))
    par = lambda a: pl.BlockSpec(a.shape, lambda b_, i: (0, 0))
    in_specs = [row(x.shape[2]) for x in xs] + [par(w) for w in ws] + [row(D_MODEL), par(g), par(bb)]
    args = list(xs) + list(ws) + [h, g, bb]
    if w_router is None:
        out_specs = [row(D_MODEL), row(D_MODEL)]
        out_shape = [jax.ShapeDtypeStruct((b, lp, D_MODEL), F32), jax.ShapeDtypeStruct((b, lp, D_MODEL), BF16)]
        scratch = []
        sem = ("parallel", "parallel")
    else:
        in_specs.append(par(w_router))
        args.append(w_router)
        out_specs = [row(D_MODEL), row(LANE), row(LANE), pl.BlockSpec((1, LANE), lambda b_, i: (0, 0))]
        out_shape = [jax.ShapeDtypeStruct((b, lp, D_MODEL), F32), jax.ShapeDtypeStruct((b, lp, LANE), F32),
                     jax.ShapeDtypeStruct((b, lp, LANE), jnp.int32), jax.ShapeDtypeStruct((1, LANE), jnp.int32)]
        scratch = [pltpu.VMEM((1, LANE), F32)]
        sem = ("arbitrary", "arbitrary")
    return pl.pallas_call(
        functools.partial(_proj_ln_body, n_in, w_router is not None),
        grid=(b, lp // tm),
        in_specs=in_specs, out_specs=out_specs, out_shape=out_shape, scratch_shapes=scratch,
        compiler_params=_cparams(*sem),
        name="out_proj_layernorm",
    )(*args)


def _ffn_body(pad, xb_ref, h_ref, wg_ref, wu_ref, wd_ref, g_ref, b_ref, o_ref, acc_ref):
    f = pl.program_id(2)
    tm = xb_ref.shape[1]

    @pl.when(f == 0)
    def _():
        acc_ref[...] = jnp.zeros(acc_ref.shape, F32)

    x = xb_ref[0]
    act = _silu(_dot(x, wg_ref[...])) * _dot(x, wu_ref[...])
    acc_ref[...] += _dot(act.astype(BF16), wd_ref[...])

    @pl.when(f == pl.num_programs(2) - 1)
    def _():
        out = _layer_norm(DEEPNORM_ALPHA * h_ref[0] + acc_ref[...], g_ref[...], b_ref[...])
        if pad:
            pos = pl.program_id(1) * tm + lax.broadcasted_iota(jnp.int32, (tm, 1), 0)
            out = jnp.where(pos >= pad, out, 0.0)
        o_ref[0] = out.astype(o_ref.dtype)


def _ffn(xb, h, w_gu, w_down, g, bb, tm, out_dtype, zero_pad):
    b, lp, _ = h.shape
    nf = D_FF // FF_TILE
    row = pl.BlockSpec((1, tm, D_MODEL), lambda b_, i, f: (b_, i, 0))
    par = lambda a: pl.BlockSpec(a.shape, lambda b_, i, f: (0, 0))
    return pl.pallas_call(
        functools.partial(_ffn_body, zero_pad),
        grid=(b, lp // tm, nf),
        in_specs=[row, row,
                  pl.BlockSpec((D_MODEL, FF_TILE), lambda b_, i, f: (0, f)),
                  pl.BlockSpec((D_MODEL, FF_TILE), lambda b_, i, f: (0, f + nf)),
                  pl.BlockSpec((FF_TILE, D_MODEL), lambda b_, i, f: (f, 0)),
                  par(g), par(bb)],
        out_specs=row,
        out_shape=jax.ShapeDtypeStruct((b, lp, D_MODEL), out_dtype),
        scratch_shapes=[pltpu.VMEM((tm, D_MODEL), F32)],
        compiler_params=_cparams("parallel", "parallel", "arbitrary"),
        name="swiglu_layernorm",
    )(xb, h, w_gu, w_gu, w_down, g, bb)


def _row_copy(src, dst, sem):
    return pltpu.make_async_copy(src, dst, sem)


def _dispatch_body(dest_ref, x_ref, zeros_hbm, o_hbm, sem):
    del zeros_hbm
    tm = x_ref.shape[1]

    def issue(r, c):
        for k in range(2):
            _row_copy(x_ref.at[0, pl.ds(r, 1)], o_hbm.at[pl.ds(dest_ref[0, 0, 2 * r + k], 1)], sem).start()
        return c

    lax.fori_loop(0, tm, issue, 0, unroll=8)
    for k in range(2):
        _row_copy(x_ref.at[0], o_hbm.at[pl.ds(0, tm)], sem).wait()


def _dispatch(h, dest, n_rows):
    b, lp, _ = h.shape
    tm = dest.shape[2] // 2
    nt = lp // tm
    return pl.pallas_call(
        _dispatch_body,
        grid=(b * nt,),
        in_specs=[pl.BlockSpec((1, 1, 2 * tm), lambda i: (i, 0, 0), memory_space=pltpu.SMEM),
                  pl.BlockSpec((1, tm, D_MODEL), lambda i: (i // nt, i % nt, 0)),
                  pl.BlockSpec(memory_space=pl.ANY)],
        out_specs=pl.BlockSpec(memory_space=pl.ANY),
        out_shape=jax.ShapeDtypeStruct((n_rows, D_MODEL), F32),
        input_output_aliases={2: 0},
        scratch_shapes=[pltpu.SemaphoreType.DMA(())],
        compiler_params=_cparams("arbitrary"),
        name="moe_dispatch",
    )(dest, h, jnp.zeros((n_rows, D_MODEL), F32))


def _experts_body(exp_ref, used_ref, x_ref, wg_ref, wu_ref, wd_ref, o_ref, acc_ref):
    t = pl.program_id(0)
    f = pl.program_id(1)
    last = f == pl.num_programs(1) - 1

    @pl.when(used_ref[t] > 0)
    def _():
        @pl.when(f == 0)
        def _():
            acc_ref[...] = jnp.zeros(acc_ref.shape, F32)

        x = x_ref[...].astype(BF16)
        act = _silu(_dot(x, wg_ref[0])) * _dot(x, wu_ref[0])
        acc_ref[...] += _dot(act.astype(BF16), wd_ref[0])

        @pl.when(last)
        def _():
            o_ref[...] = acc_ref[...]

    @pl.when((used_ref[t] == 0) & last)
    def _():
        o_ref[...] = jnp.zeros(o_ref.shape, F32)


def _experts(xs, exp, used, w_gu, w_down, tg):
    nf = D_FF // FF_TILE
    n_tiles = exp.shape[0]
    ff = lambda t, f, used_: jnp.where(used_[t] > 0, f, nf - 1)
    grid_spec = pltpu.PrefetchScalarGridSpec(
        num_scalar_prefetch=2,
        grid=(n_tiles, nf),
        in_specs=[pl.BlockSpec((tg, D_MODEL), lambda t, f, exp_, used_: (t, 0)),
                  pl.BlockSpec((1, D_MODEL, FF_TILE), lambda t, f, exp_, used_: (exp_[t], 0, ff(t, f, used_))),
                  pl.BlockSpec((1, D_MODEL, FF_TILE), lambda t, f, exp_, used_: (exp_[t], 0, ff(t, f, used_) + nf)),
                  pl.BlockSpec((1, FF_TILE, D_MODEL), lambda t, f, exp_, used_: (exp_[t], ff(t, f, used_), 0))],
        out_specs=pl.BlockSpec((tg, D_MODEL), lambda t, f, exp_, used_: (t, 0)),
        scratch_shapes=[pltpu.VMEM((tg, D_MODEL), F32)],
    )
    return pl.pallas_call(
        _experts_body,
        grid_spec=grid_spec,
        out_shape=jax.ShapeDtypeStruct(xs.shape, F32),
        compiler_params=_cparams("arbitrary", "arbitrary"),
        name="moe_experts",
    )(exp, used, xs, w_gu, w_gu, w_down)


def _combine_body(dest_ref, gate_ref, h_ref, g_ref, b_ref, y_hbm, o_ref, buf_ref, sem):
    tm = h_ref.shape[1]

    def issue(r, c):
        for k in range(2):
            _row_copy(y_hbm.at[pl.ds(dest_ref[0, 0, 2 * r + k], 1)], buf_ref.at[k, pl.ds(r, 1)], sem).start()
        return c

    lax.fori_loop(0, tm, issue, 0, unroll=8)
    for k in range(2):
        _row_copy(y_hbm.at[pl.ds(0, tm)], buf_ref.at[k], sem).wait()
    gates = gate_ref[0]
    y = buf_ref[0] * gates[:, 0:1] + buf_ref[1] * gates[:, 1:2]
    o_ref[0] = _layer_norm(DEEPNORM_ALPHA * h_ref[0] + y, g_ref[...], b_ref[...])


def _combine(ys, dest, gates, h, g, bb):
    b, lp, _ = h.shape
    tm = dest.shape[2] // 2
    nt = lp // tm
    row = lambda w: pl.BlockSpec((1, tm, w), lambda i: (i // nt, i % nt, 0))
    par = lambda a: pl.BlockSpec(a.shape, lambda i: (0, 0))
    return pl.pallas_call(
        _combine_body,
        grid=(b * nt,),
        in_specs=[pl.BlockSpec((1, 1, 2 * tm), lambda i: (i, 0, 0), memory_space=pltpu.SMEM),
                  row(LANE), row(D_MODEL), par(g), par(bb), pl.BlockSpec(memory_space=pl.ANY)],
        out_specs=row(D_MODEL),
        out_shape=jax.ShapeDtypeStruct((b, lp, D_MODEL), F32),
        scratch_shapes=[pltpu.VMEM((2, tm, D_MODEL), F32), pltpu.SemaphoreType.DMA(())],
        compiler_params=_cparams("arbitrary"),
        name="moe_combine_layernorm",
    )(dest, gates, h, g, bb, ys)


def _moe(h, gates, dest_lanes, counts, w_gu, w_down, g, bb, tm):
    b, lp, _ = h.shape
    tg = EXPERT_TILE
    n_tiles = -(-2 * b * lp // tg) + N_EXPERTS
    cnt = counts[0, :N_EXPERTS]
    tiles = (cnt + tg - 1) // tg
    ends = jnp.cumsum(tiles)
    group_row0 = (ends - tiles) * tg
    dest = (group_row0[dest_lanes[:, :, 0:2]] + dest_lanes[:, :, 2:4]).reshape(b * (lp // tm), 1, 2 * tm)
    t = jnp.arange(n_tiles, dtype=jnp.int32)
    used = (t < ends[-1]).astype(jnp.int32)
    exp = jnp.minimum(jnp.searchsorted(ends, jnp.minimum(t, ends[-1] - 1), side="right"),
                      N_EXPERTS - 1).astype(jnp.int32)
    xs = _dispatch(h, dest, n_tiles * tg)
    ys = _experts(xs, exp, used, w_gu, w_down, tg)
    return _combine(ys, dest, gates, h, g, bb)


def _rep(v):
    return jnp.repeat(v.astype(F32), HEAD_DIM)[None, :]


def _row(v):
    return v.astype(F32).reshape(1, -1)


def kernel(x, meta, ev_w_in, ev_conv_w, ev_conv_b, ev_dt_bias, ev_a_log, ev_d_skip, ev_ssm_norm, ev_shift_mu, ev_w0, ev_w_up, ev_a0, ev_a_up, ev_g_up, ev_k_k, ev_k_a, ev_r_k, ev_lnx_g, ev_lnx_b, ev_w_out, ev_ln1_g, ev_ln1_b, ev_ffn_w_gu, ev_ffn_w_down, ev_ln2_g, ev_ln2_b, od_w_in, od_conv_w, od_conv_b, od_gx_w, od_gx_b, od_ga_w, od_ga_b, od_lambda, od_w_out, od_ln1_g, od_ln1_b, od_router, od_exp_w_gu, od_exp_w_down, od_ln2_g, od_ln2_b):
    b, seq, d = x.shape
    assert d == D_MODEL
    l = seq + N_META
    pad = (-l) % CHUNK
    lp = l + pad
    assert lp % TIME_BLOCK == 0, lp
    tm = _row_tile(lp)

    h = jnp.concatenate([jnp.zeros((b, pad, d), x.dtype),
                         jnp.broadcast_to(meta.astype(x.dtype)[None], (b, N_META, d)), x], axis=1)
    hb = h.astype(BF16)

    i = 0
    w_in = ev_w_in[i]
    o1 = D_MODEL
    o2 = o1 + SSM_CONV_DIM
    o3 = o2 + N_HEADS
    z = _mm(hb, w_in[:, :o1].astype(BF16), F32, tm)
    xbc = _mm(hb, w_in[:, o1:o2].astype(BF16), F32, tm)
    dtx = _mm(hb, jnp.repeat(w_in[:, o2:o3], HEAD_DIM, axis=1).astype(BF16), F32, tm)
    cols = _mm(hb, w_in[:, o3:].astype(BF16), F32, tm)

    y_a = _ssd(z, xbc, dtx, ev_conv_w[i], _row(ev_conv_b[i]), _rep(ev_dt_bias[i]), _rep(ev_a_log[i]),
               _rep(ev_d_skip[i]), _row(ev_ssm_norm[i]), pad)

    zeros64 = jnp.zeros((64, D_MODEL), F32)
    wup = jnp.concatenate([ev_w_up[i], zeros64], axis=0).astype(BF16)
    aup = jnp.concatenate([zeros64, ev_a_up[i]], axis=0).astype(BF16)
    y_b = _rwkv(cols, _row(ev_shift_mu[i]), _row(ev_w0[i]), wup, _row(ev_a0[i]), aup, ev_g_up[i].astype(BF16),
                _row(ev_k_k[i]), _row(ev_k_a[i]), _row(ev_r_k[i]), _row(ev_lnx_g[i]), _row(ev_lnx_b[i]))

    w_out = ev_w_out[i].astype(BF16)
    h, hb = _proj_ln([y_a, y_b], [w_out[:D_MODEL], w_out[D_MODEL:]], h, _row(ev_ln1_g[i]), _row(ev_ln1_b[i]), tm)
    h = _ffn(hb, h, ev_ffn_w_gu[i].astype(BF16), ev_ffn_w_down[i].astype(BF16),
             _row(ev_ln2_g[i]), _row(ev_ln2_b[i]), tm, F32, pad)
    hb = h.astype(BF16)

    proj = _mm(hb, od_w_in[i].astype(BF16), F32, tm)
    y_c = _rglru(proj, od_conv_w[i], _row(od_conv_b[i]), od_gx_w[i].astype(BF16), _row(od_gx_b[i]),
                 od_ga_w[i].astype(BF16), _row(od_ga_b[i]), _row(od_lambda[i]), pad)
    w_router = jnp.pad(od_router[i], ((0, 0), (0, LANE - N_EXPERTS)))
    h, gates, dest, counts = _proj_ln([y_c], [od_w_out[i].astype(BF16)], h, _row(od_ln1_g[i]), _row(od_ln1_b[i]), tm,
                                      w_router=w_router)
    h = _moe(h, gates, dest, counts, od_exp_w_gu[i].astype(BF16), od_exp_w_down[i].astype(BF16),
             _row(od_ln2_g[i]), _row(od_ln2_b[i]), tm)
    return h[:, pad + N_META:]
```

```python
import functools
import math

import jax
import jax.numpy as jnp
from jax import lax
from jax.experimental import pallas as pl
from jax.experimental.pallas import tpu as pltpu

F32 = jnp.float32
BF16 = jnp.bfloat16
HIGHEST = lax.Precision.HIGHEST

D_MODEL = 1024
N_META = 16
CHUNK = 64
HEAD_DIM = 64
N_HEADS = D_MODEL // HEAD_DIM
SSM_GROUPS = 2
SSM_STATE = 128
SSM_CONV_DIM = D_MODEL + 2 * SSM_GROUPS * SSM_STATE
CONV_K = 4
RWKV_LORA = 256
RWKV_COLS = 3 * D_MODEL + RWKV_LORA
RWKV_LN_EPS = 64e-5
LRU_BLOCKS = 8
LRU_BLOCK = D_MODEL // LRU_BLOCKS
LRU_C = 8.0
D_FF = 2816
N_EXPERTS = 8
DEPTH = 2
DEEPNORM_ALPHA = (2 * DEPTH) ** 0.25
LN_EPS = 1e-5

LANE = 128
SUBLANES = 8
HALO = 8
GROUP_HEADS = 4
GROUP_W = GROUP_HEADS * HEAD_DIM
TIME_BLOCK = 192
FF_TILE = 256
EXPERT_TILE = 512
EXPERT_FF_TILE = 1408
VMEM_LIMIT = 56 * 1024 * 1024


def _cparams(*sem):
    return pltpu.CompilerParams(dimension_semantics=sem, vmem_limit_bytes=VMEM_LIMIT)


def _row_tile(lp):
    best = 16
    for t in range(16, min(lp, 1376) + 1, 16):
        if lp % t == 0:
            best = t
    return best


def _col_tile(n):
    for t in (512, 1664, 256, 128):
        if n % t == 0:
            return t
    raise ValueError(n)


def _sigmoid(x):
    return 1.0 / (1.0 + jnp.exp(-x))


def _silu(x):
    return x * _sigmoid(x)


def _softplus(x):
    return jnp.maximum(x, 0.0) + jnp.log(1.0 + jnp.exp(-jnp.abs(x)))


def _dot(a, b):
    return jnp.dot(a, b, preferred_element_type=F32)


def _dot_nt(a, b):
    return lax.dot_general(a, b, (((1,), (1,)), ((), ())), preferred_element_type=F32)


def _dot_tn(a, b):
    return lax.dot_general(a, b, (((0,), (0,)), ((), ())), preferred_element_type=F32)


def _dot_exact(a, b):
    return jnp.dot(a, b, preferred_element_type=F32, precision=HIGHEST)


def _layer_norm(x, g, b):
    mu = jnp.mean(x, axis=-1, keepdims=True)
    xc = x - mu
    var = jnp.mean(xc * xc, axis=-1, keepdims=True)
    return xc * lax.rsqrt(var + LN_EPS) * g + b


def _mm_body(x_ref, w_ref, o_ref):
    o_ref[0] = _dot(x_ref[0], w_ref[...]).astype(o_ref.dtype)


def _mm(x, w, out_dtype, tm):
    b, lp, k = x.shape
    n = w.shape[1]
    tn = _col_tile(n)
    return pl.pallas_call(
        _mm_body,
        grid=(b, lp // tm, n // tn),
        in_specs=[pl.BlockSpec((1, tm, k), lambda b_, i, j: (b_, i, 0)),
                  pl.BlockSpec((k, tn), lambda b_, i, j: (0, j))],
        out_specs=pl.BlockSpec((1, tm, tn), lambda b_, i, j: (b_, i, j)),
        out_shape=jax.ShapeDtypeStruct((b, lp, n), out_dtype),
        compiler_params=_cparams("parallel", "parallel", "arbitrary"),
        name="dense_matmul",
    )(x, w)


def _chunk_consts():
    row = lax.broadcasted_iota(jnp.int32, (CHUNK, GROUP_W), 0)
    col = lax.broadcasted_iota(jnp.int32, (CHUNK, GROUP_W), 1) % HEAD_DIM
    r2 = lax.broadcasted_iota(jnp.int32, (GROUP_W, GROUP_W), 0) // HEAD_DIM
    c2 = lax.broadcasted_iota(jnp.int32, (GROUP_W, GROUP_W), 1) // HEAD_DIM
    tri_r = lax.broadcasted_iota(jnp.int32, (CHUNK, CHUNK), 0)
    tri_c = lax.broadcasted_iota(jnp.int32, (CHUNK, CHUNK), 1)
    tri = (tri_c <= tri_r).astype(F32)
    return row, col, (r2 == c2), tri


def _block_diag(y, bd_mask):
    return jnp.where(bd_mask, jnp.concatenate([y] * GROUP_HEADS, axis=0), jnp.zeros((), y.dtype))


def _shift_halo(ext_ref, blk, first):
    tb = blk.shape[0]

    @pl.when(first)
    def _():
        ext_ref[0:HALO, :] = jnp.zeros((HALO, ext_ref.shape[1]), F32)

    @pl.when(jnp.logical_not(first))
    def _():
        ext_ref[0:HALO, :] = ext_ref[tb:tb + HALO, :]

    ext_ref[HALO:HALO + tb, :] = blk


def _ssd_body(pad, z_ref, xbc_ref, dtx_ref, cw_ref, cb_ref, dtb_ref, alog_ref, dsk_ref, nw_ref,
              y_ref, ext_ref, xc_ref, dt_ref, yb_ref, h_ref):
    tb = xbc_ref.shape[1]
    j = pl.program_id(1)

    @pl.when(j == 0)
    def _():
        h_ref[...] = jnp.zeros(h_ref.shape, F32)

    _shift_halo(ext_ref, xbc_ref[0], j == 0)
    acc = jnp.broadcast_to(cb_ref[...], (tb, SSM_CONV_DIM))
    for k in range(CONV_K):
        acc = acc + cw_ref[k:k + 1, :] * ext_ref[pl.ds(HALO - (CONV_K - 1) + k, tb), :]
    xc_ref[...] = _silu(acc)

    pos = j * tb + lax.broadcasted_iota(jnp.int32, (tb, 1), 0)
    dt_ref[...] = jnp.where(pos >= pad, _softplus(dtx_ref[0] + dtb_ref[...]), 0.0)

    row, col, bd_mask, tri = _chunk_consts()
    row_w = jnp.concatenate([row] * (D_MODEL // GROUP_W), axis=1)
    col_w = jnp.concatenate([col] * (D_MODEL // GROUP_W), axis=1)
    a_neg = -jnp.exp(alog_ref[...])
    gw = D_MODEL // SSM_GROUPS

    def chunk(c, carry):
        rows = pl.ds(pl.multiple_of(c * CHUNK, CHUNK), CHUNK)
        xs = xc_ref[rows, 0:D_MODEL]
        dt = dt_ref[rows, :]
        xdt = xs * dt
        acs = _dot_exact(tri, dt * a_neg)
        acs_t = jnp.sum(jnp.where(row_w == col_w, acs, 0.0), axis=0, keepdims=True)
        decay = jnp.where(row_w >= col_w, jnp.exp(jnp.minimum(acs - acs_t, 0.0)), 0.0)
        e_acs = jnp.exp(acs)
        last = acs[CHUNK - 1:CHUNK, :]
        xend = xdt * jnp.exp(last - acs)
        e_last = jnp.exp(last)
        groups = range(SSM_GROUPS)
        gsl = [slice(g * gw, (g + 1) * gw) for g in groups]
        bg = [xc_ref[rows, D_MODEL + g * SSM_STATE:D_MODEL + (g + 1) * SSM_STATE].astype(BF16) for g in groups]
        cg = [xc_ref[rows, D_MODEL + (SSM_GROUPS + g) * SSM_STATE:
                     D_MODEL + (SSM_GROUPS + g + 1) * SSM_STATE].astype(BF16) for g in groups]
        scores = [_dot_nt(cg[g], jnp.concatenate([bg[g]] * GROUP_HEADS, axis=0)) for g in groups]
        hg = [h_ref[:, gsl[g]] for g in groups]
        y_off = [_dot(cg[g], hg[g].astype(BF16)) * e_acs[:, gsl[g]] for g in groups]
        st = [_dot_tn(bg[g], xend[:, gsl[g]].astype(BF16)) for g in groups]
        for g in groups:
            h_ref[:, gsl[g]] = hg[g] * e_last[:, gsl[g]] + st[g]
        for g in groups:
            for t in range(gw // GROUP_W):
                lo = g * gw + t * GROUP_W
                m = (scores[g] * decay[:, lo:lo + GROUP_W]).astype(BF16)
                xbd = _block_diag(xdt[:, lo:lo + GROUP_W].astype(BF16), bd_mask)
                yb_ref[rows, lo:lo + GROUP_W] = (_dot(m, xbd) + y_off[g][:, t * GROUP_W:(t + 1) * GROUP_W])
        return carry

    lax.fori_loop(0, tb // CHUNK, chunk, 0)

    y = (yb_ref[...] + dsk_ref[...] * xc_ref[:, 0:D_MODEL]) * _silu(z_ref[0])
    outs = []
    for g in range(SSM_GROUPS):
        yg = y[:, g * gw:(g + 1) * gw]
        ms = jnp.mean(yg * yg, axis=-1, keepdims=True)
        outs.append(yg * lax.rsqrt(ms + LN_EPS) * nw_ref[:, g * gw:(g + 1) * gw])
    y_ref[0] = jnp.concatenate(outs, axis=1).astype(y_ref.dtype)


def _ssd(z, xbc, dtx, cw, cb, dtb, alog, dsk, nw, pad):
    b, lp, _ = z.shape
    tb = TIME_BLOCK
    blk = lambda w: pl.BlockSpec((1, tb, w), lambda b_, j: (b_, j, 0))
    par = lambda a: pl.BlockSpec(a.shape, lambda b_, j: (0, 0))
    return pl.pallas_call(
        functools.partial(_ssd_body, pad),
        grid=(b, lp // tb),
        in_specs=[blk(D_MODEL), blk(SSM_CONV_DIM), blk(D_MODEL)] + [par(a) for a in (cw, cb, dtb, alog, dsk, nw)],
        out_specs=blk(D_MODEL),
        out_shape=jax.ShapeDtypeStruct((b, lp, D_MODEL), BF16),
        scratch_shapes=[pltpu.VMEM((tb + HALO, SSM_CONV_DIM), F32),
                        pltpu.VMEM((tb, SSM_CONV_DIM), F32),
                        pltpu.VMEM((tb, D_MODEL), F32),
                        pltpu.VMEM((tb, D_MODEL), F32),
                        pltpu.VMEM((SSM_STATE, D_MODEL), F32)],
        compiler_params=_cparams("parallel", "arbitrary"),
        name="ssd_branch",
    )(z, xbc, dtx, cw, cb, dtb, alog, dsk, nw)


def _head_sum(x, ones_bd):
    outs = []
    for i in range(D_MODEL // LANE):
        xi = x[:, i * LANE:(i + 1) * LANE]
        hi = xi.astype(BF16)
        lo = (xi - hi.astype(F32)).astype(BF16)
        outs.append(_dot(hi, ones_bd) + _dot(lo, ones_bd))
    return jnp.concatenate(outs, axis=1)


def _rwkv_body(cols_ref, mu_ref, w0_ref, wup_ref, a0_ref, aup_ref, gup_ref, kk_ref, ka_ref, rk_ref,
               lng_ref, lnb_ref, y_ref, ext_ref, r_s, k_s, v_s, kk_s, akk_s, lw_s, o_s, s_ref):
    tb = cols_ref.shape[1]
    j = pl.program_id(1)

    @pl.when(j == 0)
    def _():
        s_ref[...] = jnp.zeros(s_ref.shape, F32)

    _shift_halo(ext_ref, cols_ref[0], j == 0)
    cur = ext_ref[HALO:HALO + tb, :]
    prev = ext_ref[pl.ds(HALO - 1, tb), :]
    mixed = cur + (prev - cur) * mu_ref[...]

    lane2 = lax.broadcasted_iota(jnp.int32, (LANE, LANE), 1) // HEAD_DIM
    row2 = lax.broadcasted_iota(jnp.int32, (LANE, LANE), 0) // HEAD_DIM
    ones_bd = (lane2 == row2).astype(BF16)

    r = mixed[:, 0:D_MODEL]
    k = mixed[:, D_MODEL:2 * D_MODEL]
    v = mixed[:, 2 * D_MODEL:3 * D_MODEL]
    lora_wa = mixed[:, 3 * D_MODEL:3 * D_MODEL + LANE]
    lora_g = mixed[:, 3 * D_MODEL + LANE:3 * D_MODEL + 2 * LANE]
    zw = w0_ref[...] + _dot(jnp.tanh(lora_wa).astype(BF16), wup_ref[...])
    a = _sigmoid(a0_ref[...] + _dot(lora_wa.astype(BF16), aup_ref[...]))
    gate = _dot(_sigmoid(lora_g).astype(BF16), gup_ref[...])
    kk = k * kk_ref[...]
    kk = kk * lax.rsqrt(jnp.maximum(_head_sum(kk * kk, ones_bd), 1e-24))
    kmod = k * (1.0 + (a - 1.0) * ka_ref[...])
    r_s[...] = r
    k_s[...] = kmod
    v_s[...] = v
    kk_s[...] = kk
    akk_s[...] = a * kk
    lw_s[...] = -math.exp(-0.5) * _sigmoid(zw)

    row, col, bd_mask, tri = _chunk_consts()
    eye = row == col
    strict = row > col
    incl = row >= col
    lvl_masks = []
    s = 2
    while s < CHUNK:
        lvl_masks.append((row // (2 * s) == col // (2 * s)) & ((row // s) % 2 == 1) & ((col // s) % 2 == 0))
        s *= 2
    pair = (row == col + 1) & (row % 2 == 1)

    def bdmm(x, y):
        return _dot(x.astype(BF16), _block_diag(y.astype(BF16), bd_mask))

    def chunk(c, carry):
        rows = pl.ds(pl.multiple_of(c * CHUNK, CHUNK), CHUNK)
        lw = lw_s[rows, :]
        lc = _dot_exact(tri, lw)
        e_p = jnp.exp(lc)
        e_m = jnp.exp(-lc)
        kkc = kk_s[rows, :]
        at = jnp.exp(lc - lw) * kkc
        bt = -(akk_s[rows, :] * e_m)
        kt = k_s[rows, :] * e_m
        rt = r_s[rows, :] * e_p
        p_last = e_p[CHUNK - 1:CHUNK, :]
        bp = bt * p_last
        kp = kt * p_last
        vc = v_s[rows, :]
        groups = range(D_MODEL // GROUP_W)
        sls = [slice(g * GROUP_W, (g + 1) * GROUP_W) for g in groups]
        v_g = [vc[:, sl].astype(BF16) for sl in sls]
        lhs = [jnp.concatenate([at[:, sl].astype(BF16), rt[:, sl].astype(BF16)], axis=0) for sl in sls]
        g_b = [_dot_nt(lhs[g], _block_diag(bt[:, sls[g]].astype(BF16), bd_mask)) for g in groups]
        g_k = [_dot_nt(lhs[g], _block_diag(kt[:, sls[g]].astype(BF16), bd_mask)) for g in groups]
        a_ab = [jnp.where(strict, g_b[g][0:CHUNK], 0.0) for g in groups]
        a_rb = [jnp.where(incl, g_b[g][CHUNK:], 0.0) for g in groups]
        a_k = [jnp.where(jnp.concatenate([strict, incl], axis=0), g_k[g], 0.0) for g in groups]
        inv = [jnp.where(eye, 1.0, 0.0) + jnp.where(pair, a_ab[g], 0.0) for g in groups]
        for m in lvl_masks:
            low = [bdmm(inv[g], jnp.where(m, a_ab[g], 0.0)) for g in groups]
            inv = [inv[g] + bdmm(low[g], inv[g]) for g in groups]
        s_q = [_dot_nt(lhs[g], s_ref[g].astype(BF16)) for g in groups]
        a_v = [bdmm(a_k[g], v_g[g]) for g in groups]
        u = [bdmm(inv[g], s_q[g][0:CHUNK] + a_v[g][0:CHUNK]) for g in groups]
        y = [s_q[g][CHUNK:] + a_v[g][CHUNK:] + bdmm(a_rb[g], u[g]) for g in groups]
        upd = [_dot_tn(jnp.concatenate([u[g].astype(BF16), v_g[g]], axis=0),
                       jnp.concatenate([bp[:, sls[g]].astype(BF16), kp[:, sls[g]].astype(BF16)], axis=0))
               for g in groups]
        for g in groups:
            o_s[rows, sls[g]] = y[g]
            s_ref[g] = s_ref[g] * p_last[:, sls[g]] + jnp.where(bd_mask, upd[g], 0.0)
        return carry

    lax.fori_loop(0, tb // CHUNK, chunk, 0)

    o = o_s[...]
    mean = _head_sum(o, ones_bd) * (1.0 / HEAD_DIM)
    oc = o - mean
    var = _head_sum(oc * oc, ones_bd) * (1.0 / HEAD_DIM)
    o = oc * lax.rsqrt(var + RWKV_LN_EPS) * lng_ref[...] + lnb_ref[...]
    rr = r_s[...]
    bonus = _head_sum(rr * k_s[...] * rk_ref[...], ones_bd)
    o = o + bonus * v_s[...]
    y_ref[0] = (o * gate).astype(y_ref.dtype)


def _rwkv(cols, mu, w0, wup, a0, aup, gup, kk, ka, rk, lng, lnb):
    b, lp, _ = cols.shape
    tb = TIME_BLOCK
    par = lambda a: pl.BlockSpec(a.shape, lambda b_, j: (0, 0))
    params = (mu, w0, wup, a0, aup, gup, kk, ka, rk, lng, lnb)
    return pl.pallas_call(
        _rwkv_body,
        grid=(b, lp // tb),
        in_specs=[pl.BlockSpec((1, tb, RWKV_COLS), lambda b_, j: (b_, j, 0))] + [par(a) for a in params],
        out_specs=pl.BlockSpec((1, tb, D_MODEL), lambda b_, j: (b_, j, 0)),
        out_shape=jax.ShapeDtypeStruct((b, lp, D_MODEL), BF16),
        scratch_shapes=[pltpu.VMEM((tb + HALO, RWKV_COLS), F32)]
        + [pltpu.VMEM((tb, D_MODEL), F32) for _ in range(7)]
        + [pltpu.VMEM((D_MODEL // GROUP_W, GROUP_W, GROUP_W), F32)],
        compiler_params=_cparams("parallel", "arbitrary"),
        name="rwkv7_branch",
    )(cols, *params)


def _gelu_tanh(x):
    return 0.5 * x * (1.0 + jnp.tanh(math.sqrt(2.0 / math.pi) * (x + 0.044715 * (x * x * x))))


def _rglru_body(pad, gb_ref, xr_ref, cw_ref, cb_ref, gxw_ref, gxb_ref, gaw_ref, gab_ref, lam_ref,
                y_ref, ext_ref, hc_ref):
    tb = xr_ref.shape[1]
    j = pl.program_id(1)

    @pl.when(j == 0)
    def _():
        hc_ref[...] = jnp.zeros(hc_ref.shape, F32)

    _shift_halo(ext_ref, xr_ref[0], j == 0)
    xf = jnp.broadcast_to(cb_ref[...], (tb, D_MODEL))
    for k in range(CONV_K):
        xf = xf + cw_ref[k:k + 1, :] * ext_ref[pl.ds(HALO - (CONV_K - 1) + k, tb), :]

    gx, ga = [], []
    for hblk in range(LRU_BLOCKS):
        xb = xf[:, hblk * LRU_BLOCK:(hblk + 1) * LRU_BLOCK].astype(BF16)
        gx.append(_dot(xb, gxw_ref[hblk]))
        ga.append(_dot(xb, gaw_ref[hblk]))
    gate_x = _sigmoid(jnp.concatenate(gx, axis=1) + gxb_ref[...])
    gate_a = _sigmoid(jnp.concatenate(ga, axis=1) + gab_ref[...])
    log_a = -LRU_C * gate_a * _softplus(-lam_ref[...])
    a = jnp.exp(log_a)
    u = jnp.sqrt(1.0 - jnp.exp(2.0 * log_a)) * (gate_x * xf)
    row = lax.broadcasted_iota(jnp.int32, (tb, 1), 0)
    u = jnp.where(j * tb + row >= pad, u, 0.0)

    a = a.reshape(tb // SUBLANES, SUBLANES, D_MODEL)
    u = u.reshape(tb // SUBLANES, SUBLANES, D_MODEL)
    sub = lax.broadcasted_iota(jnp.int32, (1, SUBLANES, 1), 1)
    d = 1
    while d < SUBLANES:
        keep = sub >= d
        a_sh = jnp.where(keep, pltpu.roll(a, d, 1), 1.0)
        u_sh = jnp.where(keep, pltpu.roll(u, d, 1), 0.0)
        u = u + a * u_sh
        a = a * a_sh
        d *= 2
    h = hc_ref[...]
    hs = []
    for grp in range(tb // SUBLANES):
        hs.append(a[grp] * h + u[grp])
        h = hs[-1][SUBLANES - 1:SUBLANES, :]
    hc_ref[...] = h
    y_ref[0] = (jnp.concatenate(hs, axis=0) * _gelu_tanh(gb_ref[0])).astype(y_ref.dtype)


def _rglru(proj, cw, cb, gxw, gxb, gaw, gab, lam, pad):
    b, lp, _ = proj.shape
    tb = TIME_BLOCK
    par2 = lambda a: pl.BlockSpec(a.shape, lambda b_, j: (0, 0))
    par3 = lambda a: pl.BlockSpec(a.shape, lambda b_, j: (0, 0, 0))
    return pl.pallas_call(
        functools.partial(_rglru_body, pad),
        grid=(b, lp // tb),
        in_specs=[pl.BlockSpec((1, tb, D_MODEL), lambda b_, j: (b_, j, 0)),
                  pl.BlockSpec((1, tb, D_MODEL), lambda b_, j: (b_, j, 1)),
                  par2(cw), par2(cb), par3(gxw), par2(gxb), par3(gaw), par2(gab), par2(lam)],
        out_specs=pl.BlockSpec((1, tb, D_MODEL), lambda b_, j: (b_, j, 0)),
        out_shape=jax.ShapeDtypeStruct((b, lp, D_MODEL), BF16),
        scratch_shapes=[pltpu.VMEM((tb + HALO, D_MODEL), F32), pltpu.VMEM((1, D_MODEL), F32)],
        compiler_params=_cparams("parallel", "arbitrary"),
        name="rglru_branch",
    )(proj, proj, cw, cb, gxw, gxb, gaw, gab, lam)


def _proj_ln_body(n_in, with_router, *refs):
    xs = refs[:n_in]
    ws = refs[n_in:2 * n_in]
    h_ref, g_ref, b_ref = refs[2 * n_in:2 * n_in + 3]
    rest = refs[2 * n_in + 3:]
    mix = _dot(xs[0][0], ws[0][...])
    for x_ref, w_ref in zip(xs[1:], ws[1:]):
        mix = mix + _dot(x_ref[0], w_ref[...])
    out = _layer_norm(DEEPNORM_ALPHA * h_ref[0] + mix, g_ref[...], b_ref[...])
    if not with_router:
        o_ref, ob_ref = rest
        ob_ref[0] = out.astype(BF16)
    else:
        wr_ref, o_ref, gate_ref, dest_ref, cnt_ref, base_ref = rest
        tm = out.shape[0]
        first = (pl.program_id(0) == 0) & (pl.program_id(1) == 0)

        @pl.when(first)
        def _():
            base_ref[...] = jnp.zeros(base_ref.shape, F32)

        logits = _dot_exact(out, wr_ref[...])
        lane = lax.broadcasted_iota(jnp.int32, logits.shape, 1)
        logits = jnp.where(lane < N_EXPERTS, logits, -jnp.inf)
        m1 = jnp.max(logits, axis=-1, keepdims=True)
        i1 = jnp.min(jnp.where(logits == m1, lane, LANE), axis=-1, keepdims=True)
        rest_l = jnp.where(lane == i1, -jnp.inf, logits)
        m2 = jnp.max(rest_l, axis=-1, keepdims=True)
        i2 = jnp.min(jnp.where(rest_l == m2, lane, LANE), axis=-1, keepdims=True)
        e2 = jnp.exp(m2 - m1)
        gate_ref[0] = jnp.where(lane == 0, 1.0 / (1.0 + e2), 0.0) + jnp.where(lane == 1, e2 / (1.0 + e2), 0.0)
        hot1 = lane == i1
        hot2 = lane == i2
        both = jnp.where(hot1 | hot2, 1.0, 0.0)
        tri_r = lax.broadcasted_iota(jnp.int32, (tm, tm), 0)
        tri_c = lax.broadcasted_iota(jnp.int32, (tm, tm), 1)
        before = _dot(jnp.where(tri_c < tri_r, 1.0, 0.0).astype(BF16), both.astype(BF16)) + base_ref[...]
        rank1 = jnp.sum(jnp.where(hot1, before, 0.0), axis=-1, keepdims=True).astype(jnp.int32)
        rank2 = jnp.sum(jnp.where(hot2, before, 0.0), axis=-1, keepdims=True).astype(jnp.int32)
        dest_ref[0] = (jnp.where(lane == 0, i1, 0) + jnp.where(lane == 1, i2, 0)
                       + jnp.where(lane == 2, rank1, 0) + jnp.where(lane == 3, rank2, 0))
        base_ref[...] += jnp.sum(both, axis=0, keepdims=True)
        cnt_ref[...] = base_ref[...].astype(jnp.int32)
    o_ref[0] = out


def _proj_ln(xs, ws, h, g, bb, tm, w_router=None):
    b, lp, _ = h.shape
    n_in = len(xs)
    row = lambda w: pl.BlockSpec((1, tm, w), lambda b_, i: (b_, i, 0))
    par = lambda a: pl.BlockSpec(a.shape, lambda b_, i: (0, 0))
    in_specs = [row(x.shape[2]) for x in xs] + [par(w) for w in ws] + [row(D_MODEL), par(g), par(bb)]
    args = list(xs) + list(ws) + [h, g, bb]
    if w_router is None:
        out_specs = [row(D_MODEL), row(D_MODEL)]
        out_shape = [jax.ShapeDtypeStruct((b, lp, D_MODEL), F32), jax.ShapeDtypeStruct((b, lp, D_MODEL), BF16)]
        scratch = []
        sem = ("parallel", "parallel")
    else:
        in_specs.append(par(w_router))
        args.append(w_router)
        out_specs = [row(D_MODEL), row(LANE), row(LANE), pl.BlockSpec((1, LANE), lambda b_, i: (0, 0))]
        out_shape = [jax.ShapeDtypeStruct((b, lp, D_MODEL), F32), jax.ShapeDtypeStruct((b, lp, LANE), F32),
                     jax.ShapeDtypeStruct((b, lp, LANE), jnp.int32), jax.ShapeDtypeStruct((1, LANE), jnp.int32)]
        scratch = [pltpu.VMEM((1, LANE), F32)]
        sem = ("arbitrary", "arbitrary")
    return pl.pallas_call(
        functools.partial(_proj_ln_body, n_in, w_router is not None),
        grid=(b, lp // tm),
        in_specs=in_specs, out_specs=out_specs, out_shape=out_shape, scratch_shapes=scratch,
        compiler_params=_cparams(*sem),
        name="out_proj_layernorm",
    )(*args)


def _ffn_body(pad, xb_ref, h_ref, wg_ref, wu_ref, wd_ref, g_ref, b_ref, o_ref, acc_ref):
    f = pl.program_id(2)
    tm = xb_ref.shape[1]

    @pl.when(f == 0)
    def _():
        acc_ref[...] = jnp.zeros(acc_ref.shape, F32)

    x = xb_ref[0]
    act = _silu(_dot(x, wg_ref[...])) * _dot(x, wu_ref[...])
    acc_ref[...] += _dot(act.astype(BF16), wd_ref[...])

    @pl.when(f == pl.num_programs(2) - 1)
    def _():
        out = _layer_norm(DEEPNORM_ALPHA * h_ref[0] + acc_ref[...], g_ref[...], b_ref[...])
        if pad:
            pos = pl.program_id(1) * tm + lax.broadcasted_iota(jnp.int32, (tm, 1), 0)
            out = jnp.where(pos >= pad, out, 0.0)
        o_ref[0] = out.astype(o_ref.dtype)


def _ffn(xb, h, w_gu, w_down, g, bb, tm, out_dtype, zero_pad):
    b, lp, _ = h.shape
    nf = D_FF // FF_TILE
    row = pl.BlockSpec((1, tm, D_MODEL), lambda b_, i, f: (b_, i, 0))
    par = lambda a: pl.BlockSpec(a.shape, lambda b_, i, f: (0, 0))
    return pl.pallas_call(
        functools.partial(_ffn_body, zero_pad),
        grid=(b, lp // tm, nf),
        in_specs=[row, row,
                  pl.BlockSpec((D_MODEL, FF_TILE), lambda b_, i, f: (0, f)),
                  pl.BlockSpec((D_MODEL, FF_TILE), lambda b_, i, f: (0, f + nf)),
                  pl.BlockSpec((FF_TILE, D_MODEL), lambda b_, i, f: (f, 0)),
                  par(g), par(bb)],
        out_specs=row,
        out_shape=jax.ShapeDtypeStruct((b, lp, D_MODEL), out_dtype),
        scratch_shapes=[pltpu.VMEM((tm, D_MODEL), F32)],
        compiler_params=_cparams("parallel", "parallel", "arbitrary"),
        name="swiglu_layernorm",
    )(xb, h, w_gu, w_gu, w_down, g, bb)


def _row_copy(src, dst, sem):
    return pltpu.make_async_copy(src, dst, sem)


def _dispatch_body(dest_ref, x_ref, zeros_hbm, o_hbm, sem):
    del zeros_hbm
    tm = x_ref.shape[1]

    def issue(r, c):
        for k in range(2):
            _row_copy(x_ref.at[0, pl.ds(r, 1)], o_hbm.at[pl.ds(dest_ref[0, 0, 2 * r + k], 1)], sem).start()
        return c

    lax.fori_loop(0, tm, issue, 0, unroll=8)
    for k in range(2):
        _row_copy(x_ref.at[0], o_hbm.at[pl.ds(0, tm)], sem).wait()


def _dispatch(h, dest, n_rows):
    b, lp, _ = h.shape
    tm = dest.shape[2] // 2
    nt = lp // tm
    return pl.pallas_call(
        _dispatch_body,
        grid=(b * nt,),
        in_specs=[pl.BlockSpec((1, 1, 2 * tm), lambda i: (i, 0, 0), memory_space=pltpu.SMEM),
                  pl.BlockSpec((1, tm, D_MODEL), lambda i: (i // nt, i % nt, 0)),
                  pl.BlockSpec(memory_space=pl.ANY)],
        out_specs=pl.BlockSpec(memory_space=pl.ANY),
        out_shape=jax.ShapeDtypeStruct((n_rows, D_MODEL), F32),
        input_output_aliases={2: 0},
        scratch_shapes=[pltpu.SemaphoreType.DMA(())],
        compiler_params=_cparams("arbitrary"),
        name="moe_dispatch",
    )(dest, h, jnp.zeros((n_rows, D_MODEL), F32))


def _experts_body(exp_ref, used_ref, x_ref, wg_ref, wu_ref, wd_ref, o_ref, acc_ref):
    t = pl.program_id(0)
    f = pl.program_id(1)
    last = f == pl.num_programs(1) - 1

    @pl.when(used_ref[t] > 0)
    def _():
        @pl.when(f == 0)
        def _():
            acc_ref[...] = jnp.zeros(acc_ref.shape, F32)

        x = x_ref[...].astype(BF16)
        act = _silu(_dot(x, wg_ref[0])) * _dot(x, wu_ref[0])
        acc_ref[...] += _dot(act.astype(BF16), wd_ref[0])

        @pl.when(last)
        def _():
            o_ref[...] = acc_ref[...]

    @pl.when((used_ref[t] == 0) & last)
    def _():
        o_ref[...] = jnp.zeros(o_ref.shape, F32)


def _experts(xs, exp, used, w_gu, w_down, tg):
    tf = EXPERT_FF_TILE
    nf = D_FF // tf
    n_tiles = exp.shape[0]
    ff = lambda t, f, used_: jnp.where(used_[t] > 0, f, nf - 1)
    grid_spec = pltpu.PrefetchScalarGridSpec(
        num_scalar_prefetch=2,
        grid=(n_tiles, nf),
        in_specs=[pl.BlockSpec((tg, D_MODEL), lambda t, f, exp_, used_: (t, 0)),
                  pl.BlockSpec((1, D_MODEL, tf), lambda t, f, exp_, used_: (exp_[t], 0, ff(t, f, used_))),
                  pl.BlockSpec((1, D_MODEL, tf), lambda t, f, exp_, used_: (exp_[t], 0, ff(t, f, used_) + nf)),
                  pl.BlockSpec((1, tf, D_MODEL), lambda t, f, exp_, used_: (exp_[t], ff(t, f, used_), 0))],
        out_specs=pl.BlockSpec((tg, D_MODEL), lambda t, f, exp_, used_: (t, 0)),
        scratch_shapes=[pltpu.VMEM((tg, D_MODEL), F32)],
    )
    return pl.pallas_call(
        _experts_body,
        grid_spec=grid_spec,
        out_shape=jax.ShapeDtypeStruct(xs.shape, F32),
        compiler_params=_cparams("arbitrary", "arbitrary"),
        name="moe_experts",
    )(exp, used, xs, w_gu, w_gu, w_down)


def _combine_body(dest_ref, gate_ref, h_ref, g_ref, b_ref, y_hbm, o_ref, buf_ref, sem):
    tm = h_ref.shape[1]

    def issue(r, c):
        for k in range(2):
            _row_copy(y_hbm.at[pl.ds(dest_ref[0, 0, 2 * r + k], 1)], buf_ref.at[k, pl.ds(r, 1)], sem).start()
        return c

    lax.fori_loop(0, tm, issue, 0, unroll=8)
    for k in range(2):
        _row_copy(y_hbm.at[pl.ds(0, tm)], buf_ref.at[k], sem).wait()
    gates = gate_ref[0]
    y = buf_ref[0] * gates[:, 0:1] + buf_ref[1] * gates[:, 1:2]
    o_ref[0] = _layer_norm(DEEPNORM_ALPHA * h_ref[0] + y, g_ref[...], b_ref[...])


def _combine(ys, dest, gates, h, g, bb):
    b, lp, _ = h.shape
    tm = dest.shape[2] // 2
    nt = lp // tm
    row = lambda w: pl.BlockSpec((1, tm, w), lambda i: (i // nt, i % nt, 0))
    par = lambda a: pl.BlockSpec(a.shape, lambda i: (0, 0))
    return pl.pallas_call(
        _combine_body,
        grid=(b * nt,),
        in_specs=[pl.BlockSpec((1, 1, 2 * tm), lambda i: (i, 0, 0), memory_space=pltpu.SMEM),
                  row(LANE), row(D_MODEL), par(g), par(bb), pl.BlockSpec(memory_space=pl.ANY)],
        out_specs=row(D_MODEL),
        out_shape=jax.ShapeDtypeStruct((b, lp, D_MODEL), F32),
        scratch_shapes=[pltpu.VMEM((2, tm, D_MODEL), F32), pltpu.SemaphoreType.DMA(())],
        compiler_params=_cparams("arbitrary"),
        name="moe_combine_layernorm",
    )(dest, gates, h, g, bb, ys)


def _moe(h, gates, dest_lanes, counts, w_gu, w_down, g, bb, tm):
    b, lp, _ = h.shape
    tg = EXPERT_TILE
    n_tiles = -(-2 * b * lp // tg) + N_EXPERTS
    cnt = counts[0, :N_EXPERTS]
    tiles = (cnt + tg - 1) // tg
    ends = jnp.cumsum(tiles)
    group_row0 = (ends - tiles) * tg
    hot = dest_lanes[:, :, 0:2, None] == jnp.arange(N_EXPERTS, dtype=jnp.int32)
    dest = (jnp.sum(jnp.where(hot, group_row0, 0), axis=-1) + dest_lanes[:, :, 2:4]).reshape(b * (lp // tm), 1, 2 * tm)
    t = jnp.arange(n_tiles, dtype=jnp.int32)
    used = (t < ends[-1]).astype(jnp.int32)
    exp = jnp.minimum(jnp.searchsorted(ends, jnp.minimum(t, ends[-1] - 1), side="right"),
                      N_EXPERTS - 1).astype(jnp.int32)
    xs = _dispatch(h, dest, n_tiles * tg)
    ys = _experts(xs, exp, used, w_gu, w_down, tg)
    return _combine(ys, dest, gates, h, g, bb)


def _rep(v):
    return jnp.repeat(v.astype(F32), HEAD_DIM)[None, :]


def _row(v):
    return v.astype(F32).reshape(1, -1)


def kernel(x, meta, ev_w_in, ev_conv_w, ev_conv_b, ev_dt_bias, ev_a_log, ev_d_skip, ev_ssm_norm, ev_shift_mu, ev_w0, ev_w_up, ev_a0, ev_a_up, ev_g_up, ev_k_k, ev_k_a, ev_r_k, ev_lnx_g, ev_lnx_b, ev_w_out, ev_ln1_g, ev_ln1_b, ev_ffn_w_gu, ev_ffn_w_down, ev_ln2_g, ev_ln2_b, od_w_in, od_conv_w, od_conv_b, od_gx_w, od_gx_b, od_ga_w, od_ga_b, od_lambda, od_w_out, od_ln1_g, od_ln1_b, od_router, od_exp_w_gu, od_exp_w_down, od_ln2_g, od_ln2_b):
    b, seq, d = x.shape
    assert d == D_MODEL
    l = seq + N_META
    pad = (-l) % CHUNK
    lp = l + pad
    assert lp % TIME_BLOCK == 0, lp
    tm = _row_tile(lp)

    h = jnp.concatenate([jnp.zeros((b, pad, d), x.dtype),
                         jnp.broadcast_to(meta.astype(x.dtype)[None], (b, N_META, d)), x], axis=1)
    hb = h.astype(BF16)

    i = 0
    w_in = ev_w_in[i]
    o1 = D_MODEL
    o2 = o1 + SSM_CONV_DIM
    o3 = o2 + N_HEADS
    z = _mm(hb, w_in[:, :o1].astype(BF16), F32, tm)
    xbc = _mm(hb, w_in[:, o1:o2].astype(BF16), F32, tm)
    dtx = _mm(hb, jnp.repeat(w_in[:, o2:o3], HEAD_DIM, axis=1).astype(BF16), F32, tm)
    cols = _mm(hb, w_in[:, o3:].astype(BF16), F32, tm)

    y_a = _ssd(z, xbc, dtx, ev_conv_w[i], _row(ev_conv_b[i]), _rep(ev_dt_bias[i]), _rep(ev_a_log[i]),
               _rep(ev_d_skip[i]), _row(ev_ssm_norm[i]), pad)

    zeros64 = jnp.zeros((64, D_MODEL), F32)
    wup = jnp.concatenate([ev_w_up[i], zeros64], axis=0).astype(BF16)
    aup = jnp.concatenate([zeros64, ev_a_up[i]], axis=0).astype(BF16)
    y_b = _rwkv(cols, _row(ev_shift_mu[i]), _row(ev_w0[i]), wup, _row(ev_a0[i]), aup, ev_g_up[i].astype(BF16),
                _row(ev_k_k[i]), _row(ev_k_a[i]), _row(ev_r_k[i]), _row(ev_lnx_g[i]), _row(ev_lnx_b[i]))

    w_out = ev_w_out[i].astype(BF16)
    h, hb = _proj_ln([y_a, y_b], [w_out[:D_MODEL], w_out[D_MODEL:]], h, _row(ev_ln1_g[i]), _row(ev_ln1_b[i]), tm)
    h = _ffn(hb, h, ev_ffn_w_gu[i].astype(BF16), ev_ffn_w_down[i].astype(BF16),
             _row(ev_ln2_g[i]), _row(ev_ln2_b[i]), tm, F32, pad)
    hb = h.astype(BF16)

    proj = _mm(hb, od_w_in[i].astype(BF16), F32, tm)
    y_c = _rglru(proj, od_conv_w[i], _row(od_conv_b[i]), od_gx_w[i].astype(BF16), _row(od_gx_b[i]),
                 od_ga_w[i].astype(BF16), _row(od_ga_b[i]), _row(od_lambda[i]), pad)
    w_router = jnp.pad(od_router[i], ((0, 0), (0, LANE - N_EXPERTS)))
    h, gates, dest, counts = _proj_ln([y_c], [od_w_out[i].astype(BF16)], h, _row(od_ln1_g[i]), _row(od_ln1_b[i]), tm,
                                      w_router=w_router)
    h = _moe(h, gates, dest, counts, od_exp_w_gu[i].astype(BF16), od_exp_w_down[i].astype(BF16),
             _row(od_ln2_g[i]), _row(od_ln2_b[i]), tm)
    return h[:, pad + N_META:]
```

```python
import functools
import math

import jax
import jax.numpy as jnp
from jax import lax
from jax.experimental import pallas as pl
from jax.experimental.pallas import tpu as pltpu

F32 = jnp.float32
BF16 = jnp.bfloat16
HIGHEST = lax.Precision.HIGHEST

D_MODEL = 1024
N_META = 16
CHUNK = 64
HEAD_DIM = 64
N_HEADS = D_MODEL // HEAD_DIM
SSM_GROUPS = 2
SSM_STATE = 128
SSM_CONV_DIM = D_MODEL + 2 * SSM_GROUPS * SSM_STATE
CONV_K = 4
RWKV_LORA = 256
RWKV_COLS = 3 * D_MODEL + RWKV_LORA
RWKV_LN_EPS = 64e-5
LRU_BLOCKS = 8
LRU_BLOCK = D_MODEL // LRU_BLOCKS
LRU_C = 8.0
D_FF = 2816
N_EXPERTS = 8
DEPTH = 2
DEEPNORM_ALPHA = (2 * DEPTH) ** 0.25
LN_EPS = 1e-5

LANE = 128
SUBLANES = 8
HALO = 8
GROUP_HEADS = 4
GROUP_W = GROUP_HEADS * HEAD_DIM
TIME_BLOCK = 192
FF_TILE = 256
RANK_SPLIT = 4
EXPERT_TILE = 512
EXPERT_FF_TILE = 1408
VMEM_LIMIT = 56 * 1024 * 1024


def _cparams(*sem):
    return pltpu.CompilerParams(dimension_semantics=sem, vmem_limit_bytes=VMEM_LIMIT)


def _row_tile(lp):
    best = 16
    for t in range(16, min(lp, 1376) + 1, 16):
        if lp % t == 0:
            best = t
    return best


def _col_tile(n):
    for t in (512, 1664, 256, 128):
        if n % t == 0:
            return t
    raise ValueError(n)


def _sigmoid(x):
    return 1.0 / (1.0 + jnp.exp(-x))


def _silu(x):
    return x * _sigmoid(x)


def _softplus(x):
    return jnp.maximum(x, 0.0) + jnp.log(1.0 + jnp.exp(-jnp.abs(x)))


def _dot(a, b):
    return jnp.dot(a, b, preferred_element_type=F32)


def _dot_nt(a, b):
    return lax.dot_general(a, b, (((1,), (1,)), ((), ())), preferred_element_type=F32)


def _dot_tn(a, b):
    return lax.dot_general(a, b, (((0,), (0,)), ((), ())), preferred_element_type=F32)


def _dot_exact(a, b):
    return jnp.dot(a, b, preferred_element_type=F32, precision=HIGHEST)


def _layer_norm(x, g, b):
    mu = jnp.mean(x, axis=-1, keepdims=True)
    xc = x - mu
    var = jnp.mean(xc * xc, axis=-1, keepdims=True)
    return xc * lax.rsqrt(var + LN_EPS) * g + b


def _mm_body(x_ref, w_ref, o_ref):
    o_ref[0] = _dot(x_ref[0], w_ref[...]).astype(o_ref.dtype)


def _mm(x, w, out_dtype, tm):
    b, lp, k = x.shape
    n = w.shape[1]
    tn = _col_tile(n)
    return pl.pallas_call(
        _mm_body,
        grid=(b, lp // tm, n // tn),
        in_specs=[pl.BlockSpec((1, tm, k), lambda b_, i, j: (b_, i, 0)),
                  pl.BlockSpec((k, tn), lambda b_, i, j: (0, j))],
        out_specs=pl.BlockSpec((1, tm, tn), lambda b_, i, j: (b_, i, j)),
        out_shape=jax.ShapeDtypeStruct((b, lp, n), out_dtype),
        compiler_params=_cparams("parallel", "parallel", "arbitrary"),
        name="dense_matmul",
    )(x, w)


def _chunk_consts():
    row = lax.broadcasted_iota(jnp.int32, (CHUNK, GROUP_W), 0)
    col = lax.broadcasted_iota(jnp.int32, (CHUNK, GROUP_W), 1) % HEAD_DIM
    r2 = lax.broadcasted_iota(jnp.int32, (GROUP_W, GROUP_W), 0) // HEAD_DIM
    c2 = lax.broadcasted_iota(jnp.int32, (GROUP_W, GROUP_W), 1) // HEAD_DIM
    tri_r = lax.broadcasted_iota(jnp.int32, (CHUNK, CHUNK), 0)
    tri_c = lax.broadcasted_iota(jnp.int32, (CHUNK, CHUNK), 1)
    tri = (tri_c <= tri_r).astype(F32)
    return row, col, (r2 == c2), tri


def _block_diag(y, bd_mask):
    return jnp.where(bd_mask, jnp.concatenate([y] * GROUP_HEADS, axis=0), jnp.zeros((), y.dtype))


def _shift_halo(ext_ref, blk, first):
    tb = blk.shape[0]

    @pl.when(first)
    def _():
        ext_ref[0:HALO, :] = jnp.zeros((HALO, ext_ref.shape[1]), F32)

    @pl.when(jnp.logical_not(first))
    def _():
        ext_ref[0:HALO, :] = ext_ref[tb:tb + HALO, :]

    ext_ref[HALO:HALO + tb, :] = blk


def _ssd_body(pad, z_ref, xbc_ref, dtx_ref, cw_ref, cb_ref, dtb_ref, alog_ref, dsk_ref, nw_ref,
              y_ref, ext_ref, xc_ref, dt_ref, yb_ref, h_ref):
    tb = xbc_ref.shape[1]
    j = pl.program_id(1)

    @pl.when(j == 0)
    def _():
        h_ref[...] = jnp.zeros(h_ref.shape, F32)

    _shift_halo(ext_ref, xbc_ref[0], j == 0)
    acc = jnp.broadcast_to(cb_ref[...], (tb, SSM_CONV_DIM))
    for k in range(CONV_K):
        acc = acc + cw_ref[k:k + 1, :] * ext_ref[pl.ds(HALO - (CONV_K - 1) + k, tb), :]
    xc_ref[...] = _silu(acc)

    pos = j * tb + lax.broadcasted_iota(jnp.int32, (tb, 1), 0)
    dt_ref[...] = jnp.where(pos >= pad, _softplus(dtx_ref[0] + dtb_ref[...]), 0.0)

    row, col, bd_mask, tri = _chunk_consts()
    row_w = jnp.concatenate([row] * (D_MODEL // GROUP_W), axis=1)
    col_w = jnp.concatenate([col] * (D_MODEL // GROUP_W), axis=1)
    a_neg = -jnp.exp(alog_ref[...])
    gw = D_MODEL // SSM_GROUPS

    def chunk(c, carry):
        rows = pl.ds(pl.multiple_of(c * CHUNK, CHUNK), CHUNK)
        xs = xc_ref[rows, 0:D_MODEL]
        dt = dt_ref[rows, :]
        xdt = xs * dt
        acs = _dot_exact(tri, dt * a_neg)
        acs_t = jnp.sum(jnp.where(row_w == col_w, acs, 0.0), axis=0, keepdims=True)
        decay = jnp.where(row_w >= col_w, jnp.exp(jnp.minimum(acs - acs_t, 0.0)), 0.0)
        e_acs = jnp.exp(acs)
        last = acs[CHUNK - 1:CHUNK, :]
        xend = xdt * jnp.exp(last - acs)
        e_last = jnp.exp(last)
        groups = range(SSM_GROUPS)
        gsl = [slice(g * gw, (g + 1) * gw) for g in groups]
        bg = [xc_ref[rows, D_MODEL + g * SSM_STATE:D_MODEL + (g + 1) * SSM_STATE].astype(BF16) for g in groups]
        cg = [xc_ref[rows, D_MODEL + (SSM_GROUPS + g) * SSM_STATE:
                     D_MODEL + (SSM_GROUPS + g + 1) * SSM_STATE].astype(BF16) for g in groups]
        scores = [_dot_nt(cg[g], jnp.concatenate([bg[g]] * GROUP_HEADS, axis=0)) for g in groups]
        hg = [h_ref[:, gsl[g]] for g in groups]
        y_off = [_dot(cg[g], hg[g].astype(BF16)) * e_acs[:, gsl[g]] for g in groups]
        st = [_dot_tn(bg[g], xend[:, gsl[g]].astype(BF16)) for g in groups]
        for g in groups:
            h_ref[:, gsl[g]] = hg[g] * e_last[:, gsl[g]] + st[g]
        for g in groups:
            for t in range(gw // GROUP_W):
                lo = g * gw + t * GROUP_W
                m = (scores[g] * decay[:, lo:lo + GROUP_W]).astype(BF16)
                xbd = _block_diag(xdt[:, lo:lo + GROUP_W].astype(BF16), bd_mask)
                yb_ref[rows, lo:lo + GROUP_W] = (_dot(m, xbd) + y_off[g][:, t * GROUP_W:(t + 1) * GROUP_W])
        return carry

    lax.fori_loop(0, tb // CHUNK, chunk, 0)

    y = (yb_ref[...] + dsk_ref[...] * xc_ref[:, 0:D_MODEL]) * _silu(z_ref[0])
    outs = []
    for g in range(SSM_GROUPS):
        yg = y[:, g * gw:(g + 1) * gw]
        ms = jnp.mean(yg * yg, axis=-1, keepdims=True)
        outs.append(yg * lax.rsqrt(ms + LN_EPS) * nw_ref[:, g * gw:(g + 1) * gw])
    y_ref[0] = jnp.concatenate(outs, axis=1).astype(y_ref.dtype)


def _ssd(z, xbc, dtx, cw, cb, dtb, alog, dsk, nw, pad):
    b, lp, _ = z.shape
    tb = TIME_BLOCK
    blk = lambda w: pl.BlockSpec((1, tb, w), lambda b_, j: (b_, j, 0))
    par = lambda a: pl.BlockSpec(a.shape, lambda b_, j: (0, 0))
    return pl.pallas_call(
        functools.partial(_ssd_body, pad),
        grid=(b, lp // tb),
        in_specs=[blk(D_MODEL), blk(SSM_CONV_DIM), blk(D_MODEL)] + [par(a) for a in (cw, cb, dtb, alog, dsk, nw)],
        out_specs=blk(D_MODEL),
        out_shape=jax.ShapeDtypeStruct((b, lp, D_MODEL), BF16),
        scratch_shapes=[pltpu.VMEM((tb + HALO, SSM_CONV_DIM), F32),
                        pltpu.VMEM((tb, SSM_CONV_DIM), F32),
                        pltpu.VMEM((tb, D_MODEL), F32),
                        pltpu.VMEM((tb, D_MODEL), F32),
                        pltpu.VMEM((SSM_STATE, D_MODEL), F32)],
        compiler_params=_cparams("parallel", "arbitrary"),
        name="ssd_branch",
    )(z, xbc, dtx, cw, cb, dtb, alog, dsk, nw)


def _head_sum(x, ones_bd):
    outs = []
    for i in range(D_MODEL // LANE):
        xi = x[:, i * LANE:(i + 1) * LANE]
        hi = xi.astype(BF16)
        lo = (xi - hi.astype(F32)).astype(BF16)
        outs.append(_dot(hi, ones_bd) + _dot(lo, ones_bd))
    return jnp.concatenate(outs, axis=1)


def _rwkv_body(cols_ref, mu_ref, w0_ref, wup_ref, a0_ref, aup_ref, gup_ref, kk_ref, ka_ref, rk_ref,
               lng_ref, lnb_ref, y_ref, ext_ref, r_s, k_s, v_s, kk_s, akk_s, lw_s, o_s, s_ref):
    tb = cols_ref.shape[1]
    j = pl.program_id(1)

    @pl.when(j == 0)
    def _():
        s_ref[...] = jnp.zeros(s_ref.shape, F32)

    _shift_halo(ext_ref, cols_ref[0], j == 0)
    cur = ext_ref[HALO:HALO + tb, :]
    prev = ext_ref[pl.ds(HALO - 1, tb), :]
    mixed = cur + (prev - cur) * mu_ref[...]

    lane2 = lax.broadcasted_iota(jnp.int32, (LANE, LANE), 1) // HEAD_DIM
    row2 = lax.broadcasted_iota(jnp.int32, (LANE, LANE), 0) // HEAD_DIM
    ones_bd = (lane2 == row2).astype(BF16)

    r = mixed[:, 0:D_MODEL]
    k = mixed[:, D_MODEL:2 * D_MODEL]
    v = mixed[:, 2 * D_MODEL:3 * D_MODEL]
    lora_wa = mixed[:, 3 * D_MODEL:3 * D_MODEL + LANE]
    lora_g = mixed[:, 3 * D_MODEL + LANE:3 * D_MODEL + 2 * LANE]
    zw = w0_ref[...] + _dot(jnp.tanh(lora_wa).astype(BF16), wup_ref[...])
    a = _sigmoid(a0_ref[...] + _dot(lora_wa.astype(BF16), aup_ref[...]))
    gate = _dot(_sigmoid(lora_g).astype(BF16), gup_ref[...])
    kk = k * kk_ref[...]
    kk = kk * lax.rsqrt(jnp.maximum(_head_sum(kk * kk, ones_bd), 1e-24))
    kmod = k * (1.0 + (a - 1.0) * ka_ref[...])
    r_s[...] = r
    k_s[...] = kmod
    v_s[...] = v
    kk_s[...] = kk
    akk_s[...] = a * kk
    lw_s[...] = -math.exp(-0.5) * _sigmoid(zw)

    row, col, bd_mask, tri = _chunk_consts()
    eye = row == col
    strict = row > col
    incl = row >= col
    lvl_masks = []
    s = 2
    while s < CHUNK:
        lvl_masks.append((row // (2 * s) == col // (2 * s)) & ((row // s) % 2 == 1) & ((col // s) % 2 == 0))
        s *= 2
    pair = (row == col + 1) & (row % 2 == 1)

    def bdmm(x, y):
        return _dot(x.astype(BF16), _block_diag(y.astype(BF16), bd_mask))

    groups = range(D_MODEL // GROUP_W)
    sls = [slice(g * GROUP_W, (g + 1) * GROUP_W) for g in groups]

    def state_free(c):
        rows = slice(c * CHUNK, (c + 1) * CHUNK)
        lw = lw_s[rows, :]
        lc = _dot_exact(tri, lw)
        e_p = jnp.exp(lc)
        e_m = jnp.exp(-lc)
        kkc = kk_s[rows, :]
        at = jnp.exp(lc - lw) * kkc
        bt = -(akk_s[rows, :] * e_m)
        kt = k_s[rows, :] * e_m
        rt = r_s[rows, :] * e_p
        p_last = e_p[CHUNK - 1:CHUNK, :]
        bp = bt * p_last
        kp = kt * p_last
        vc = v_s[rows, :]
        v_g = [vc[:, sl].astype(BF16) for sl in sls]
        lhs = [jnp.concatenate([at[:, sl].astype(BF16), rt[:, sl].astype(BF16)], axis=0) for sl in sls]
        g_b = [_dot_nt(lhs[g], _block_diag(bt[:, sls[g]].astype(BF16), bd_mask)) for g in groups]
        g_k = [_dot_nt(lhs[g], _block_diag(kt[:, sls[g]].astype(BF16), bd_mask)) for g in groups]
        a_ab = [jnp.where(strict, g_b[g][0:CHUNK], 0.0) for g in groups]
        a_rb = [jnp.where(incl, g_b[g][CHUNK:], 0.0) for g in groups]
        a_k = [jnp.where(jnp.concatenate([strict, incl], axis=0), g_k[g], 0.0) for g in groups]
        a_v = [bdmm(a_k[g], v_g[g]) for g in groups]
        yield
        inv = [jnp.where(eye, 1.0, 0.0) + jnp.where(pair, a_ab[g], 0.0) for g in groups]
        for m in lvl_masks:
            low = [bdmm(inv[g], jnp.where(m, a_ab[g], 0.0)) for g in groups]
            yield
            inv = [inv[g] + bdmm(low[g], inv[g]) for g in groups]
            yield
        bkp = [jnp.concatenate([bp[:, sl].astype(BF16), kp[:, sl].astype(BF16)], axis=0) for sl in sls]
        return dict(rows=rows, lhs=lhs, v_g=v_g, a_rb=a_rb, a_v=a_v, inv=inv, bkp=bkp, p_last=p_last)

    def state_dependent(ctx):
        s_q = [_dot_nt(ctx["lhs"][g], s_ref[g].astype(BF16)) for g in groups]
        yield
        u = [bdmm(ctx["inv"][g], s_q[g][0:CHUNK] + ctx["a_v"][g][0:CHUNK]) for g in groups]
        yield
        y = [s_q[g][CHUNK:] + ctx["a_v"][g][CHUNK:] + bdmm(ctx["a_rb"][g], u[g]) for g in groups]
        upd = [_dot_tn(jnp.concatenate([u[g].astype(BF16), ctx["v_g"][g]], axis=0), ctx["bkp"][g]) for g in groups]
        for g in groups:
            o_s[ctx["rows"], sls[g]] = y[g]
            s_ref[g] = s_ref[g] * ctx["p_last"][:, sls[g]] + jnp.where(bd_mask, upd[g], 0.0)

    def emit_interleaved(first, second):
        result, live = None, [g for g in (first, second) if g is not None]
        while live:
            for gen in list(live):
                try:
                    next(gen)
                except StopIteration as stop:
                    live.remove(gen)
                    if gen is first:
                        result = stop.value
        return result

    n_chunks = tb // CHUNK
    ctx = emit_interleaved(state_free(0), None)
    for c in range(n_chunks):
        nxt = state_free(c + 1) if c + 1 < n_chunks else None
        ctx_next = emit_interleaved(nxt, state_dependent(ctx))
        ctx = ctx_next

    o = o_s[...]
    mean = _head_sum(o, ones_bd) * (1.0 / HEAD_DIM)
    oc = o - mean
    var = _head_sum(oc * oc, ones_bd) * (1.0 / HEAD_DIM)
    o = oc * lax.rsqrt(var + RWKV_LN_EPS) * lng_ref[...] + lnb_ref[...]
    rr = r_s[...]
    bonus = _head_sum(rr * k_s[...] * rk_ref[...], ones_bd)
    o = o + bonus * v_s[...]
    y_ref[0] = (o * gate).astype(y_ref.dtype)


def _rwkv(cols, mu, w0, wup, a0, aup, gup, kk, ka, rk, lng, lnb):
    b, lp, _ = cols.shape
    tb = TIME_BLOCK
    par = lambda a: pl.BlockSpec(a.shape, lambda b_, j: (0, 0))
    params = (mu, w0, wup, a0, aup, gup, kk, ka, rk, lng, lnb)
    return pl.pallas_call(
        _rwkv_body,
        grid=(b, lp // tb),
        in_specs=[pl.BlockSpec((1, tb, RWKV_COLS), lambda b_, j: (b_, j, 0))] + [par(a) for a in params],
        out_specs=pl.BlockSpec((1, tb, D_MODEL), lambda b_, j: (b_, j, 0)),
        out_shape=jax.ShapeDtypeStruct((b, lp, D_MODEL), BF16),
        scratch_shapes=[pltpu.VMEM((tb + HALO, RWKV_COLS), F32)]
        + [pltpu.VMEM((tb, D_MODEL), F32) for _ in range(7)]
        + [pltpu.VMEM((D_MODEL // GROUP_W, GROUP_W, GROUP_W), F32)],
        compiler_params=_cparams("parallel", "arbitrary"),
        name="rwkv7_branch",
    )(cols, *params)


def _gelu_tanh(x):
    return 0.5 * x * (1.0 + jnp.tanh(math.sqrt(2.0 / math.pi) * (x + 0.044715 * (x * x * x))))


def _rglru_body(pad, gb_ref, xr_ref, cw_ref, cb_ref, gxw_ref, gxb_ref, gaw_ref, gab_ref, lam_ref,
                y_ref, ext_ref, hc_ref):
    tb = xr_ref.shape[1]
    j = pl.program_id(1)

    @pl.when(j == 0)
    def _():
        hc_ref[...] = jnp.zeros(hc_ref.shape, F32)

    _shift_halo(ext_ref, xr_ref[0], j == 0)
    xf = jnp.broadcast_to(cb_ref[...], (tb, D_MODEL))
    for k in range(CONV_K):
        xf = xf + cw_ref[k:k + 1, :] * ext_ref[pl.ds(HALO - (CONV_K - 1) + k, tb), :]

    gx, ga = [], []
    for hblk in range(LRU_BLOCKS):
        xb = xf[:, hblk * LRU_BLOCK:(hblk + 1) * LRU_BLOCK].astype(BF16)
        gx.append(_dot(xb, gxw_ref[hblk]))
        ga.append(_dot(xb, gaw_ref[hblk]))
    gate_x = _sigmoid(jnp.concatenate(gx, axis=1) + gxb_ref[...])
    gate_a = _sigmoid(jnp.concatenate(ga, axis=1) + gab_ref[...])
    log_a = -LRU_C * gate_a * _softplus(-lam_ref[...])
    a = jnp.exp(log_a)
    u = jnp.sqrt(1.0 - jnp.exp(2.0 * log_a)) * (gate_x * xf)
    row = lax.broadcasted_iota(jnp.int32, (tb, 1), 0)
    u = jnp.where(j * tb + row >= pad, u, 0.0)

    a = a.reshape(tb // SUBLANES, SUBLANES, D_MODEL)
    u = u.reshape(tb // SUBLANES, SUBLANES, D_MODEL)
    sub = lax.broadcasted_iota(jnp.int32, (1, SUBLANES, 1), 1)
    d = 1
    while d < SUBLANES:
        keep = sub >= d
        a_sh = jnp.where(keep, pltpu.roll(a, d, 1), 1.0)
        u_sh = jnp.where(keep, pltpu.roll(u, d, 1), 0.0)
        u = u + a * u_sh
        a = a * a_sh
        d *= 2
    h = hc_ref[...]
    hs = []
    for grp in range(tb // SUBLANES):
        hs.append(a[grp] * h + u[grp])
        h = hs[-1][SUBLANES - 1:SUBLANES, :]
    hc_ref[...] = h
    y_ref[0] = (jnp.concatenate(hs, axis=0) * _gelu_tanh(gb_ref[0])).astype(y_ref.dtype)


def _rglru(proj, cw, cb, gxw, gxb, gaw, gab, lam, pad):
    b, lp, _ = proj.shape
    tb = TIME_BLOCK
    par2 = lambda a: pl.BlockSpec(a.shape, lambda b_, j: (0, 0))
    par3 = lambda a: pl.BlockSpec(a.shape, lambda b_, j: (0, 0, 0))
    return pl.pallas_call(
        functools.partial(_rglru_body, pad),
        grid=(b, lp // tb),
        in_specs=[pl.BlockSpec((1, tb, D_MODEL), lambda b_, j: (b_, j, 0)),
                  pl.BlockSpec((1, tb, D_MODEL), lambda b_, j: (b_, j, 1)),
                  par2(cw), par2(cb), par3(gxw), par2(gxb), par3(gaw), par2(gab), par2(lam)],
        out_specs=pl.BlockSpec((1, tb, D_MODEL), lambda b_, j: (b_, j, 0)),
        out_shape=jax.ShapeDtypeStruct((b, lp, D_MODEL), BF16),
        scratch_shapes=[pltpu.VMEM((tb + HALO, D_MODEL), F32), pltpu.VMEM((1, D_MODEL), F32)],
        compiler_params=_cparams("parallel", "arbitrary"),
        name="rglru_branch",
    )(proj, proj, cw, cb, gxw, gxb, gaw, gab, lam)


def _proj_ln_body(n_in, with_router, *refs):
    xs = refs[:n_in]
    ws = refs[n_in:2 * n_in]
    h_ref, g_ref, b_ref = refs[2 * n_in:2 * n_in + 3]
    rest = refs[2 * n_in + 3:]
    mix = _dot(xs[0][0], ws[0][...])
    for x_ref, w_ref in zip(xs[1:], ws[1:]):
        mix = mix + _dot(x_ref[0], w_ref[...])
    out = _layer_norm(DEEPNORM_ALPHA * h_ref[0] + mix, g_ref[...], b_ref[...])
    if not with_router:
        o_ref, ob_ref = rest
        ob_ref[0] = out.astype(BF16)
    else:
        wr_ref, o_ref, gate_ref, dest_ref, cnt_ref, base_ref = rest
        tm = out.shape[0]
        first = (pl.program_id(0) == 0) & (pl.program_id(1) == 0)

        @pl.when(first)
        def _():
            base_ref[...] = jnp.zeros(base_ref.shape, F32)

        o_hi = out.astype(BF16)
        o_lo = (out - o_hi.astype(F32)).astype(BF16)
        wr = wr_ref[...]
        w_hi = wr.astype(BF16)
        w_lo = (wr - w_hi.astype(F32)).astype(BF16)
        logits = _dot(o_hi, w_hi) + (_dot(o_lo, w_hi) + _dot(o_hi, w_lo))
        lane = lax.broadcasted_iota(jnp.int32, logits.shape, 1)
        logits = jnp.where(lane < N_EXPERTS, logits, -jnp.inf)
        m1 = jnp.max(logits, axis=-1, keepdims=True)
        i1 = jnp.min(jnp.where(logits == m1, lane, LANE), axis=-1, keepdims=True)
        rest_l = jnp.where(lane == i1, -jnp.inf, logits)
        m2 = jnp.max(rest_l, axis=-1, keepdims=True)
        i2 = jnp.min(jnp.where(rest_l == m2, lane, LANE), axis=-1, keepdims=True)
        e2 = jnp.exp(m2 - m1)
        gate_ref[0] = jnp.where(lane == 0, 1.0 / (1.0 + e2), 0.0) + jnp.where(lane == 1, e2 / (1.0 + e2), 0.0)
        hot1 = lane == i1
        hot2 = lane == i2
        both = jnp.where(hot1 | hot2, 1.0, 0.0)
        ts = tm // RANK_SPLIT
        tri_r = lax.broadcasted_iota(jnp.int32, (ts, ts), 0)
        tri_c = lax.broadcasted_iota(jnp.int32, (ts, ts), 1)
        tri = jnp.where(tri_c < tri_r, 1.0, 0.0).astype(BF16)
        base = base_ref[...]
        before = []
        for q in range(RANK_SPLIT):
            sub = both[q * ts:(q + 1) * ts]
            before.append(_dot(tri, sub.astype(BF16)) + base)
            base = base + jnp.sum(sub, axis=0, keepdims=True)
        before = jnp.concatenate(before, axis=0)
        rank1 = jnp.sum(jnp.where(hot1, before, 0.0), axis=-1, keepdims=True).astype(jnp.int32)
        rank2 = jnp.sum(jnp.where(hot2, before, 0.0), axis=-1, keepdims=True).astype(jnp.int32)
        dest_ref[0] = (jnp.where(lane == 0, i1, 0) + jnp.where(lane == 1, i2, 0)
                       + jnp.where(lane == 2, rank1, 0) + jnp.where(lane == 3, rank2, 0))
        base_ref[...] = base
        cnt_ref[...] = base.astype(jnp.int32)
    o_ref[0] = out


def _proj_ln(xs, ws, h, g, bb, tm, w_router=None):
    b, lp, _ = h.shape
    n_in = len(xs)
    row = lambda w: pl.BlockSpec((1, tm, w), lambda b_, i: (b_, i, 0))
    par = lambda a: pl.BlockSpec(a.shape, lambda b_, i: (0, 0))
    in_specs = [row(x.shape[2]) for x in xs] + [par(w) for w in ws] + [row(D_MODEL), par(g), par(bb)]
    args = list(xs) + list(ws) + [h, g, bb]
    if w_router is None:
        out_specs = [row(D_MODEL), row(D_MODEL)]
        out_shape = [jax.ShapeDtypeStruct((b, lp, D_MODEL), F32), jax.ShapeDtypeStruct((b, lp, D_MODEL), BF16)]
        scratch = []
        sem = ("parallel", "parallel")
    else:
        in_specs.append(par(w_router))
        args.append(w_router)
        out_specs = [row(D_MODEL), row(LANE), row(LANE), pl.BlockSpec((1, LANE), lambda b_, i: (0, 0))]
        out_shape = [jax.ShapeDtypeStruct((b, lp, D_MODEL), F32), jax.ShapeDtypeStruct((b, lp, LANE), F32),
                     jax.ShapeDtypeStruct((b, lp, LANE), jnp.int32), jax.ShapeDtypeStruct((1, LANE), jnp.int32)]
        scratch = [pltpu.VMEM((1, LANE), F32)]
        sem = ("arbitrary", "arbitrary")
    return pl.pallas_call(
        functools.partial(_proj_ln_body, n_in, w_router is not None),
        grid=(b, lp // tm),
        in_specs=in_specs, out_specs=out_specs, out_shape=out_shape, scratch_shapes=scratch,
        compiler_params=_cparams(*sem),
        name="out_proj_layernorm",
    )(*args)


def _ffn_body(pad, xb_ref, h_ref, wg_ref, wu_ref, wd_ref, g_ref, b_ref, o_ref, ob_ref, acc_ref):
    f = pl.program_id(2)
    tm = xb_ref.shape[1]

    @pl.when(f == 0)
    def _():
        acc_ref[...] = jnp.zeros(acc_ref.shape, F32)

    x = xb_ref[0]
    act = _silu(_dot(x, wg_ref[...])) * _dot(x, wu_ref[...])
    acc_ref[...] += _dot(act.astype(BF16), wd_ref[...])

    @pl.when(f == pl.num_programs(2) - 1)
    def _():
        out = _layer_norm(DEEPNORM_ALPHA * h_ref[0] + acc_ref[...], g_ref[...], b_ref[...])
        if pad:
            pos = pl.program_id(1) * tm + lax.broadcasted_iota(jnp.int32, (tm, 1), 0)
            out = jnp.where(pos >= pad, out, 0.0)
        o_ref[0] = out
        ob_ref[0] = out.astype(BF16)


def _ffn(xb, h, w_gu, w_down, g, bb, tm, zero_pad):
    b, lp, _ = h.shape
    nf = D_FF // FF_TILE
    row = pl.BlockSpec((1, tm, D_MODEL), lambda b_, i, f: (b_, i, 0))
    par = lambda a: pl.BlockSpec(a.shape, lambda b_, i, f: (0, 0))
    return pl.pallas_call(
        functools.partial(_ffn_body, zero_pad),
        grid=(b, lp // tm, nf),
        in_specs=[row, row,
                  pl.BlockSpec((D_MODEL, FF_TILE), lambda b_, i, f: (0, f)),
                  pl.BlockSpec((D_MODEL, FF_TILE), lambda b_, i, f: (0, f + nf)),
                  pl.BlockSpec((FF_TILE, D_MODEL), lambda b_, i, f: (f, 0)),
                  par(g), par(bb)],
        out_specs=[row, row],
        out_shape=[jax.ShapeDtypeStruct((b, lp, D_MODEL), F32), jax.ShapeDtypeStruct((b, lp, D_MODEL), BF16)],
        scratch_shapes=[pltpu.VMEM((tm, D_MODEL), F32)],
        compiler_params=_cparams("parallel", "parallel", "arbitrary"),
        name="swiglu_layernorm",
    )(xb, h, w_gu, w_gu, w_down, g, bb)


def _row_copy(src, dst, sem):
    return pltpu.make_async_copy(src, dst, sem)


def _dispatch_body(dest_ref, x_ref, zeros_hbm, o_hbm, sem):
    del zeros_hbm
    tm = x_ref.shape[1]

    def issue(r, c):
        for k in range(2):
            _row_copy(x_ref.at[0, pl.ds(r, 1)], o_hbm.at[pl.ds(dest_ref[0, 0, 2 * r + k], 1)], sem).start()
        return c

    lax.fori_loop(0, tm, issue, 0, unroll=8)
    for k in range(2):
        _row_copy(x_ref.at[0], o_hbm.at[pl.ds(0, tm)], sem).wait()


def _dispatch(h, dest, n_rows):
    b, lp, _ = h.shape
    tm = dest.shape[2] // 2
    nt = lp // tm
    return pl.pallas_call(
        _dispatch_body,
        grid=(b * nt,),
        in_specs=[pl.BlockSpec((1, 1, 2 * tm), lambda i: (i, 0, 0), memory_space=pltpu.SMEM),
                  pl.BlockSpec((1, tm, D_MODEL), lambda i: (i // nt, i % nt, 0)),
                  pl.BlockSpec(memory_space=pl.ANY)],
        out_specs=pl.BlockSpec(memory_space=pl.ANY),
        out_shape=jax.ShapeDtypeStruct((n_rows, D_MODEL), F32),
        input_output_aliases={2: 0},
        scratch_shapes=[pltpu.SemaphoreType.DMA(())],
        compiler_params=_cparams("arbitrary"),
        name="moe_dispatch",
    )(dest, h, jnp.zeros((n_rows, D_MODEL), F32))


def _experts_body(exp_ref, used_ref, x_ref, wg_ref, wu_ref, wd_ref, o_ref, acc_ref):
    t = pl.program_id(0)
    f = pl.program_id(1)
    last = f == pl.num_programs(1) - 1

    @pl.when(used_ref[t] > 0)
    def _():
        @pl.when(f == 0)
        def _():
            acc_ref[...] = jnp.zeros(acc_ref.shape, F32)

        x = x_ref[...].astype(BF16)
        act = _silu(_dot(x, wg_ref[0])) * _dot(x, wu_ref[0])
        acc_ref[...] += _dot(act.astype(BF16), wd_ref[0])

        @pl.when(last)
        def _():
            o_ref[...] = acc_ref[...]

    @pl.when((used_ref[t] == 0) & last)
    def _():
        o_ref[...] = jnp.zeros(o_ref.shape, F32)


def _experts(xs, exp, used, w_gu, w_down, tg):
    tf = EXPERT_FF_TILE
    nf = D_FF // tf
    n_tiles = exp.shape[0]
    ff = lambda t, f, used_: jnp.where(used_[t] > 0, f, nf - 1)
    grid_spec = pltpu.PrefetchScalarGridSpec(
        num_scalar_prefetch=2,
        grid=(n_tiles, nf),
        in_specs=[pl.BlockSpec((tg, D_MODEL), lambda t, f, exp_, used_: (t, 0)),
                  pl.BlockSpec((1, D_MODEL, tf), lambda t, f, exp_, used_: (exp_[t], 0, ff(t, f, used_))),
                  pl.BlockSpec((1, D_MODEL, tf), lambda t, f, exp_, used_: (exp_[t], 0, ff(t, f, used_) + nf)),
                  pl.BlockSpec((1, tf, D_MODEL), lambda t, f, exp_, used_: (exp_[t], ff(t, f, used_), 0))],
        out_specs=pl.BlockSpec((tg, D_MODEL), lambda t, f, exp_, used_: (t, 0)),
        scratch_shapes=[pltpu.VMEM((tg, D_MODEL), F32)],
    )
    return pl.pallas_call(
        _experts_body,
        grid_spec=grid_spec,
        out_shape=jax.ShapeDtypeStruct(xs.shape, F32),
        compiler_params=_cparams("arbitrary", "arbitrary"),
        name="moe_experts",
    )(exp, used, xs, w_gu, w_gu, w_down)


def _combine_body(row0, nt, dest_ref, gate_ref, g_ref, b_ref, h_hbm, y_hbm, o_ref, buf_ref, h_buf, sem, h_sem):
    tc = o_ref.shape[1]
    i = pl.program_id(0)
    h_copy = pltpu.make_async_copy(h_hbm.at[i // nt, pl.ds(row0 + (i % nt) * tc, tc)], h_buf, h_sem)
    h_copy.start()

    def issue(r, c):
        for k in range(2):
            _row_copy(y_hbm.at[pl.ds(dest_ref[0, 0, 2 * r + k], 1)], buf_ref.at[k, pl.ds(r, 1)], sem).start()
        return c

    lax.fori_loop(0, tc, issue, 0, unroll=8)
    for k in range(2):
        _row_copy(y_hbm.at[pl.ds(0, tc)], buf_ref.at[k], sem).wait()
    h_copy.wait()
    gates = gate_ref[0]
    y = buf_ref[0] * gates[:, 0:1] + buf_ref[1] * gates[:, 1:2]
    o_ref[0] = _layer_norm(DEEPNORM_ALPHA * h_buf[...] + y, g_ref[...], b_ref[...])


def _combine(ys, dest, gates, h, g, bb, row0, seq):
    b = h.shape[0]
    tc = dest.shape[2] // 2
    nt = seq // tc
    par = lambda a: pl.BlockSpec(a.shape, lambda i: (0, 0))
    return pl.pallas_call(
        functools.partial(_combine_body, row0, nt),
        grid=(b * nt,),
        in_specs=[pl.BlockSpec((1, 1, 2 * tc), lambda i: (i, 0, 0), memory_space=pltpu.SMEM),
                  pl.BlockSpec((1, tc, LANE), lambda i: (i // nt, i % nt, 0)),
                  par(g), par(bb), pl.BlockSpec(memory_space=pl.ANY), pl.BlockSpec(memory_space=pl.ANY)],
        out_specs=pl.BlockSpec((1, tc, D_MODEL), lambda i: (i // nt, i % nt, 0)),
        out_shape=jax.ShapeDtypeStruct((b, seq, D_MODEL), F32),
        scratch_shapes=[pltpu.VMEM((2, tc, D_MODEL), F32), pltpu.VMEM((tc, D_MODEL), F32),
                        pltpu.SemaphoreType.DMA(()), pltpu.SemaphoreType.DMA(())],
        compiler_params=_cparams("arbitrary"),
        name="moe_combine_layernorm",
    )(dest, gates, g, bb, h, ys)


def _moe(h, gates, dest_lanes, counts, w_gu, w_down, g, bb, tm, row0):
    b, lp, _ = h.shape
    seq = lp - row0
    tc = max(t for t in range(SUBLANES, min(seq, 1024) + 1, SUBLANES) if seq % t == 0)
    tg = EXPERT_TILE
    n_tiles = -(-2 * b * lp // tg) + N_EXPERTS
    cnt = counts[0, :N_EXPERTS]
    tiles = (cnt + tg - 1) // tg
    ends = jnp.cumsum(tiles)
    group_row0 = (ends - tiles) * tg
    hot = dest_lanes[:, :, 0:2, None] == jnp.arange(N_EXPERTS, dtype=jnp.int32)
    dest = jnp.sum(jnp.where(hot, group_row0, 0), axis=-1) + dest_lanes[:, :, 2:4]
    dest_frames = dest[:, row0:].reshape(b * (seq // tc), 1, 2 * tc)
    dest = dest.reshape(b * (lp // tm), 1, 2 * tm)
    t = jnp.arange(n_tiles, dtype=jnp.int32)
    used = (t < ends[-1]).astype(jnp.int32)
    exp = jnp.minimum(jnp.searchsorted(ends, jnp.minimum(t, ends[-1] - 1), side="right"),
                      N_EXPERTS - 1).astype(jnp.int32)
    xs = _dispatch(h, dest, n_tiles * tg)
    ys = _experts(xs, exp, used, w_gu, w_down, tg)
    return _combine(ys, dest_frames, gates[:, row0:], h, g, bb, row0, seq)


def _rep(v):
    return jnp.repeat(v.astype(F32), HEAD_DIM)[None, :]


def _row(v):
    return v.astype(F32).reshape(1, -1)


def kernel(x, meta, ev_w_in, ev_conv_w, ev_conv_b, ev_dt_bias, ev_a_log, ev_d_skip, ev_ssm_norm, ev_shift_mu, ev_w0, ev_w_up, ev_a0, ev_a_up, ev_g_up, ev_k_k, ev_k_a, ev_r_k, ev_lnx_g, ev_lnx_b, ev_w_out, ev_ln1_g, ev_ln1_b, ev_ffn_w_gu, ev_ffn_w_down, ev_ln2_g, ev_ln2_b, od_w_in, od_conv_w, od_conv_b, od_gx_w, od_gx_b, od_ga_w, od_ga_b, od_lambda, od_w_out, od_ln1_g, od_ln1_b, od_router, od_exp_w_gu, od_exp_w_down, od_ln2_g, od_ln2_b):
    b, seq, d = x.shape
    assert d == D_MODEL
    l = seq + N_META
    pad = (-l) % CHUNK
    lp = l + pad
    assert lp % TIME_BLOCK == 0, lp
    tm = _row_tile(lp)

    h = jnp.concatenate([jnp.zeros((b, pad, d), x.dtype),
                         jnp.broadcast_to(meta.astype(x.dtype)[None], (b, N_META, d)), x], axis=1)
    hb = h.astype(BF16)

    i = 0
    w_in = ev_w_in[i]
    o1 = D_MODEL
    o2 = o1 + SSM_CONV_DIM
    o3 = o2 + N_HEADS
    z = _mm(hb, w_in[:, :o1].astype(BF16), F32, tm)
    xbc = _mm(hb, w_in[:, o1:o2].astype(BF16), F32, tm)
    dtx = _mm(hb, jnp.repeat(w_in[:, o2:o3], HEAD_DIM, axis=1).astype(BF16), F32, tm)
    cols = _mm(hb, w_in[:, o3:].astype(BF16), F32, tm)

    y_a = _ssd(z, xbc, dtx, ev_conv_w[i], _row(ev_conv_b[i]), _rep(ev_dt_bias[i]), _rep(ev_a_log[i]),
               _rep(ev_d_skip[i]), _row(ev_ssm_norm[i]), pad)

    zeros64 = jnp.zeros((64, D_MODEL), F32)
    wup = jnp.concatenate([ev_w_up[i], zeros64], axis=0).astype(BF16)
    aup = jnp.concatenate([zeros64, ev_a_up[i]], axis=0).astype(BF16)
    y_b = _rwkv(cols, _row(ev_shift_mu[i]), _row(ev_w0[i]), wup, _row(ev_a0[i]), aup, ev_g_up[i].astype(BF16),
                _row(ev_k_k[i]), _row(ev_k_a[i]), _row(ev_r_k[i]), _row(ev_lnx_g[i]), _row(ev_lnx_b[i]))

    w_out = ev_w_out[i].astype(BF16)
    h, hb = _proj_ln([y_a, y_b], [w_out[:D_MODEL], w_out[D_MODEL:]], h, _row(ev_ln1_g[i]), _row(ev_ln1_b[i]), tm)
    h, hb = _ffn(hb, h, ev_ffn_w_gu[i].astype(BF16), ev_ffn_w_down[i].astype(BF16),
                 _row(ev_ln2_g[i]), _row(ev_ln2_b[i]), tm, pad)

    proj = _mm(hb, od_w_in[i].astype(BF16), F32, tm)
    y_c = _rglru(proj, od_conv_w[i], _row(od_conv_b[i]), od_gx_w[i].astype(BF16), _row(od_gx_b[i]),
                 od_ga_w[i].astype(BF16), _row(od_ga_b[i]), _row(od_lambda[i]), pad)
    w_router = jnp.pad(od_router[i], ((0, 0), (0, LANE - N_EXPERTS)))
    h, gates, dest, counts = _proj_ln([y_c], [od_w_out[i].astype(BF16)], h, _row(od_ln1_g[i]), _row(od_ln1_b[i]), tm,
                                      w_router=w_router)
    return _moe(h, gates, dest, counts, od_exp_w_gu[i].astype(BF16), od_exp_w_down[i].astype(BF16),
                _row(od_ln2_g[i]), _row(od_ln2_b[i]), tm, pad + N_META)
```

```python
import functools
import math

import jax
import jax.numpy as jnp
from jax import lax
from jax.experimental import pallas as pl
from jax.experimental.pallas import tpu as pltpu

F32 = jnp.float32
BF16 = jnp.bfloat16

D_MODEL = 1024
N_META = 16
CHUNK = 64
HEAD_DIM = 64
N_HEADS = D_MODEL // HEAD_DIM
SSM_GROUPS = 2
SSM_STATE = 128
SSM_CONV_DIM = D_MODEL + 2 * SSM_GROUPS * SSM_STATE
CONV_K = 4
RWKV_LORA = 256
RWKV_COLS = 3 * D_MODEL + RWKV_LORA
RWKV_LN_EPS = 64e-5
LRU_BLOCKS = 8
LRU_BLOCK = D_MODEL // LRU_BLOCKS
LRU_C = 8.0
D_FF = 2816
N_EXPERTS = 8
DEPTH = 2
DEEPNORM_ALPHA = (2 * DEPTH) ** 0.25
LN_EPS = 1e-5

LANE = 128
SUBLANES = 8
HALO = 8
GROUP_HEADS = 4
GROUP_W = GROUP_HEADS * HEAD_DIM
BD_SLOTS = 8
TIME_BLOCK = 192
FF_TILE = 1408
FFN_ROW_SPLIT = 2
RANK_SPLIT = 4
EXPERT_TILE = 512
EXPERT_FF_TILE = FF_TILE
VMEM_LIMIT = 56 * 1024 * 1024


def _cparams(*sem):
    return pltpu.CompilerParams(dimension_semantics=sem, vmem_limit_bytes=VMEM_LIMIT)


def _row_tile(lp):
    best = 16
    for t in range(16, min(lp, 1376) + 1, 16):
        if lp % t == 0:
            best = t
    return best


def _col_tile(n):
    for t in (512, 1664, 256, 128):
        if n % t == 0:
            return t
    raise ValueError(n)


def _sigmoid(x):
    return 1.0 / (1.0 + jnp.exp(-x))


def _silu(x):
    return x * _sigmoid(x)


def _softplus(x):
    return jnp.maximum(x, 0.0) + jnp.log(1.0 + jnp.exp(-jnp.abs(x)))


def _dot(a, b):
    return jnp.dot(a, b, preferred_element_type=F32)


def _dot_nt(a, b):
    return lax.dot_general(a, b, (((1,), (1,)), ((), ())), preferred_element_type=F32)


def _dot_tn(a, b):
    return lax.dot_general(a, b, (((0,), (0,)), ((), ())), preferred_element_type=F32)


def _prefix_sum(tri, x):
    hi = x.astype(BF16)
    r1 = x - hi.astype(F32)
    mid = r1.astype(BF16)
    lo = (r1 - mid.astype(F32)).astype(BF16)
    return _dot(tri, hi) + (_dot(tri, mid) + _dot(tri, lo))


def _layer_norm(x, g, b):
    mu = jnp.mean(x, axis=-1, keepdims=True)
    xc = x - mu
    var = jnp.mean(xc * xc, axis=-1, keepdims=True)
    return xc * lax.rsqrt(var + LN_EPS) * g + b


def _mm_body(x_ref, w_ref, o_ref):
    o_ref[0] = _dot(x_ref[0], w_ref[...]).astype(o_ref.dtype)


def _mm(x, w, out_dtype, tm):
    b, lp, k = x.shape
    n = w.shape[1]
    tn = _col_tile(n)
    return pl.pallas_call(
        _mm_body,
        grid=(b, lp // tm, n // tn),
        in_specs=[pl.BlockSpec((1, tm, k), lambda b_, i, j: (b_, i, 0)),
                  pl.BlockSpec((k, tn), lambda b_, i, j: (0, j))],
        out_specs=pl.BlockSpec((1, tm, tn), lambda b_, i, j: (b_, i, j)),
        out_shape=jax.ShapeDtypeStruct((b, lp, n), out_dtype),
        compiler_params=_cparams("parallel", "parallel", "arbitrary"),
        name="dense_matmul",
    )(x, w)


def _chunk_consts():
    row = lax.broadcasted_iota(jnp.int32, (CHUNK, GROUP_W), 0)
    col = lax.broadcasted_iota(jnp.int32, (CHUNK, GROUP_W), 1) % HEAD_DIM
    r2 = lax.broadcasted_iota(jnp.int32, (GROUP_W, GROUP_W), 0) // HEAD_DIM
    c2 = lax.broadcasted_iota(jnp.int32, (GROUP_W, GROUP_W), 1) // HEAD_DIM
    tri_r = lax.broadcasted_iota(jnp.int32, (CHUNK, CHUNK), 0)
    tri_c = lax.broadcasted_iota(jnp.int32, (CHUNK, CHUNK), 1)
    tri = jnp.where(tri_c <= tri_r, 1.0, 0.0).astype(BF16)
    return row, col, (r2 == c2), tri


def _block_diag(y, bd_mask):
    return jnp.where(bd_mask, jnp.concatenate([y] * GROUP_HEADS, axis=0), jnp.zeros((), y.dtype))


def _shift_halo(ext_ref, blk, first):
    tb = blk.shape[0]

    @pl.when(first)
    def _():
        ext_ref[0:HALO, :] = jnp.zeros((HALO, ext_ref.shape[1]), F32)

    @pl.when(jnp.logical_not(first))
    def _():
        ext_ref[0:HALO, :] = ext_ref[tb:tb + HALO, :]

    ext_ref[HALO:HALO + tb, :] = blk


def _ssd_body(pad, z_ref, xbc_ref, dtx_ref, cw_ref, cb_ref, dtb_ref, alog_ref, dsk_ref, nw_ref,
              y_ref, ext_ref, xc_ref, dt_ref, yb_ref, h_ref):
    tb = xbc_ref.shape[1]
    j = pl.program_id(1)

    @pl.when(j == 0)
    def _():
        h_ref[...] = jnp.zeros(h_ref.shape, F32)

    _shift_halo(ext_ref, xbc_ref[0], j == 0)
    acc = jnp.broadcast_to(cb_ref[...], (tb, SSM_CONV_DIM))
    for k in range(CONV_K):
        acc = acc + cw_ref[k:k + 1, :] * ext_ref[pl.ds(HALO - (CONV_K - 1) + k, tb), :]
    xc_ref[...] = _silu(acc)

    pos = j * tb + lax.broadcasted_iota(jnp.int32, (tb, 1), 0)
    dt_ref[...] = jnp.where(pos >= pad, _softplus(dtx_ref[0] + dtb_ref[...]), 0.0)

    row, col, bd_mask, tri = _chunk_consts()
    row_w = jnp.concatenate([row] * (D_MODEL // GROUP_W), axis=1)
    col_w = jnp.concatenate([col] * (D_MODEL // GROUP_W), axis=1)
    a_neg = -jnp.exp(alog_ref[...])
    gw = D_MODEL // SSM_GROUPS

    def chunk(c, carry):
        rows = pl.ds(pl.multiple_of(c * CHUNK, CHUNK), CHUNK)
        xs = xc_ref[rows, 0:D_MODEL]
        dt = dt_ref[rows, :]
        xdt = xs * dt
        acs = _prefix_sum(tri, dt * a_neg)
        acs_t = jnp.sum(jnp.where(row_w == col_w, acs, 0.0), axis=0, keepdims=True)
        decay = jnp.where(row_w >= col_w, jnp.exp(jnp.minimum(acs - acs_t, 0.0)), 0.0)
        e_acs = jnp.exp(acs)
        last = acs[CHUNK - 1:CHUNK, :]
        xend = xdt * jnp.exp(last - acs)
        e_last = jnp.exp(last)
        groups = range(SSM_GROUPS)
        gsl = [slice(g * gw, (g + 1) * gw) for g in groups]
        bg = [xc_ref[rows, D_MODEL + g * SSM_STATE:D_MODEL + (g + 1) * SSM_STATE].astype(BF16) for g in groups]
        cg = [xc_ref[rows, D_MODEL + (SSM_GROUPS + g) * SSM_STATE:
                     D_MODEL + (SSM_GROUPS + g + 1) * SSM_STATE].astype(BF16) for g in groups]
        scores = [_dot_nt(cg[g], jnp.concatenate([bg[g]] * GROUP_HEADS, axis=0)) for g in groups]
        hg = [h_ref[:, gsl[g]] for g in groups]
        y_off = [_dot(cg[g], hg[g].astype(BF16)) * e_acs[:, gsl[g]] for g in groups]
        st = [_dot_tn(bg[g], xend[:, gsl[g]].astype(BF16)) for g in groups]
        for g in groups:
            h_ref[:, gsl[g]] = hg[g] * e_last[:, gsl[g]] + st[g]
        for g in groups:
            for t in range(gw // GROUP_W):
                lo = g * gw + t * GROUP_W
                m = (scores[g] * decay[:, lo:lo + GROUP_W]).astype(BF16)
                xbd = _block_diag(xdt[:, lo:lo + GROUP_W].astype(BF16), bd_mask)
                yb_ref[rows, lo:lo + GROUP_W] = (_dot(m, xbd) + y_off[g][:, t * GROUP_W:(t + 1) * GROUP_W])
        return carry

    lax.fori_loop(0, tb // CHUNK, chunk, 0)

    y = (yb_ref[...] + dsk_ref[...] * xc_ref[:, 0:D_MODEL]) * _silu(z_ref[0])
    outs = []
    for g in range(SSM_GROUPS):
        yg = y[:, g * gw:(g + 1) * gw]
        ms = jnp.mean(yg * yg, axis=-1, keepdims=True)
        outs.append(yg * lax.rsqrt(ms + LN_EPS) * nw_ref[:, g * gw:(g + 1) * gw])
    y_ref[0] = jnp.concatenate(outs, axis=1).astype(y_ref.dtype)


def _ssd(z, xbc, dtx, cw, cb, dtb, alog, dsk, nw, pad):
    b, lp, _ = z.shape
    tb = TIME_BLOCK
    blk = lambda w: pl.BlockSpec((1, tb, w), lambda b_, j: (b_, j, 0))
    par = lambda a: pl.BlockSpec(a.shape, lambda b_, j: (0, 0))
    return pl.pallas_call(
        functools.partial(_ssd_body, pad),
        grid=(b, lp // tb),
        in_specs=[blk(D_MODEL), blk(SSM_CONV_DIM), blk(D_MODEL)] + [par(a) for a in (cw, cb, dtb, alog, dsk, nw)],
        out_specs=blk(D_MODEL),
        out_shape=jax.ShapeDtypeStruct((b, lp, D_MODEL), BF16),
        scratch_shapes=[pltpu.VMEM((tb + HALO, SSM_CONV_DIM), F32),
                        pltpu.VMEM((tb, SSM_CONV_DIM), F32),
                        pltpu.VMEM((tb, D_MODEL), F32),
                        pltpu.VMEM((tb, D_MODEL), F32),
                        pltpu.VMEM((SSM_STATE, D_MODEL), F32)],
        compiler_params=_cparams("parallel", "arbitrary"),
        name="ssd_branch",
    )(z, xbc, dtx, cw, cb, dtb, alog, dsk, nw)


def _head_sum(x, ones_bd):
    outs = []
    for i in range(D_MODEL // LANE):
        xi = x[:, i * LANE:(i + 1) * LANE]
        hi = xi.astype(BF16)
        lo = (xi - hi.astype(F32)).astype(BF16)
        outs.append(_dot(hi, ones_bd) + _dot(lo, ones_bd))
    return jnp.concatenate(outs, axis=1)


def _rwkv_body(cols_ref, mu_ref, w0_ref, wup_ref, a0_ref, aup_ref, gup_ref, kk_ref, ka_ref, rk_ref,
               lng_ref, lnb_ref, y_ref, ext_ref, r_s, k_s, v_s, kk_s, akk_s, lw_s, o_s, s_ref, bd_s):
    tb = cols_ref.shape[1]
    j = pl.program_id(1)

    @pl.when(j == 0)
    def _():
        s_ref[...] = jnp.zeros(s_ref.shape, F32)
        bd_s[...] = jnp.zeros(bd_s.shape, BF16)

    _shift_halo(ext_ref, cols_ref[0], j == 0)
    cur = ext_ref[HALO:HALO + tb, :]
    prev = ext_ref[pl.ds(HALO - 1, tb), :]
    mixed = cur + (prev - cur) * mu_ref[...]

    lane2 = lax.broadcasted_iota(jnp.int32, (LANE, LANE), 1) // HEAD_DIM
    row2 = lax.broadcasted_iota(jnp.int32, (LANE, LANE), 0) // HEAD_DIM
    ones_bd = (lane2 == row2).astype(BF16)

    r = mixed[:, 0:D_MODEL]
    k = mixed[:, D_MODEL:2 * D_MODEL]
    v = mixed[:, 2 * D_MODEL:3 * D_MODEL]
    lora_wa = mixed[:, 3 * D_MODEL:3 * D_MODEL + LANE]
    lora_g = mixed[:, 3 * D_MODEL + LANE:3 * D_MODEL + 2 * LANE]
    zw = w0_ref[...] + _dot(jnp.tanh(lora_wa).astype(BF16), wup_ref[...])
    a = _sigmoid(a0_ref[...] + _dot(lora_wa.astype(BF16), aup_ref[...]))
    gate = _dot(_sigmoid(lora_g).astype(BF16), gup_ref[...])
    kk = k * kk_ref[...]
    kk = kk * lax.rsqrt(jnp.maximum(_head_sum(kk * kk, ones_bd), 1e-24))
    kmod = k * (1.0 + (a - 1.0) * ka_ref[...])
    r_s[...] = r
    k_s[...] = kmod
    v_s[...] = v
    kk_s[...] = kk
    akk_s[...] = a * kk
    lw_s[...] = -math.exp(-0.5) * _sigmoid(zw)

    row, col, bd_mask, tri = _chunk_consts()
    eye = row == col
    strict = row > col
    incl = row >= col
    lvl_masks = []
    s = 2
    while s < CHUNK:
        lvl_masks.append((row // (2 * s) == col // (2 * s)) & ((row // s) % 2 == 1) & ((col // s) % 2 == 0))
        s *= 2
    pair = (row == col + 1) & (row % 2 == 1)

    bd_slot = [0]

    def block_diag(y):
        return _block_diag(y, bd_mask)

    def bdmm(x, y):
        return _dot(x.astype(BF16), block_diag(y.astype(BF16)))

    groups = range(D_MODEL // GROUP_W)
    sls = [slice(g * GROUP_W, (g + 1) * GROUP_W) for g in groups]

    def state_free(c):
        rows = slice(c * CHUNK, (c + 1) * CHUNK)
        lw = lw_s[rows, :]
        lc = _prefix_sum(tri, lw)
        e_p = jnp.exp(lc)
        e_m = jnp.exp(-lc)
        kkc = kk_s[rows, :]
        at = jnp.exp(lc - lw) * kkc
        bt = -(akk_s[rows, :] * e_m)
        kt = k_s[rows, :] * e_m
        rt = r_s[rows, :] * e_p
        p_last = e_p[CHUNK - 1:CHUNK, :]
        bp = bt * p_last
        kp = kt * p_last
        vc = v_s[rows, :]
        v_g = [vc[:, sl].astype(BF16) for sl in sls]
        lhs = [jnp.concatenate([at[:, sl].astype(BF16), rt[:, sl].astype(BF16)], axis=0) for sl in sls]
        g_b = [_dot_nt(lhs[g], block_diag(bt[:, sls[g]].astype(BF16))) for g in groups]
        g_k = [_dot_nt(lhs[g], block_diag(kt[:, sls[g]].astype(BF16))) for g in groups]
        a_ab = [jnp.where(strict, g_b[g][0:CHUNK], 0.0) for g in groups]
        a_rb = [jnp.where(incl, g_b[g][CHUNK:], 0.0) for g in groups]
        a_k = [jnp.where(jnp.concatenate([strict, incl], axis=0), g_k[g], 0.0) for g in groups]
        a_v = [bdmm(a_k[g], v_g[g]) for g in groups]
        yield
        inv = [jnp.where(eye, 1.0, 0.0) + jnp.where(pair, a_ab[g], 0.0) for g in groups]
        for m in lvl_masks:
            low = [bdmm(inv[g], jnp.where(m, a_ab[g], 0.0)) for g in groups]
            yield
            inv = [inv[g] + bdmm(low[g], inv[g]) for g in groups]
            yield
        bkp = [jnp.concatenate([bp[:, sl].astype(BF16), kp[:, sl].astype(BF16)], axis=0) for sl in sls]
        return dict(rows=rows, lhs=lhs, v_g=v_g, a_rb=a_rb, a_v=a_v, inv=inv, bkp=bkp, p_last=p_last)

    def state_dependent(ctx):
        s_q = [_dot_nt(ctx["lhs"][g], s_ref[g].astype(BF16)) for g in groups]
        yield
        u = [bdmm(ctx["inv"][g], s_q[g][0:CHUNK] + ctx["a_v"][g][0:CHUNK]) for g in groups]
        yield
        y = [s_q[g][CHUNK:] + ctx["a_v"][g][CHUNK:] + bdmm(ctx["a_rb"][g], u[g]) for g in groups]
        upd = [_dot_tn(jnp.concatenate([u[g].astype(BF16), ctx["v_g"][g]], axis=0), ctx["bkp"][g]) for g in groups]
        for g in groups:
            o_s[ctx["rows"], sls[g]] = y[g]
            s_ref[g] = s_ref[g] * ctx["p_last"][:, sls[g]] + jnp.where(bd_mask, upd[g], 0.0)

    def emit_interleaved(first, second):
        result, live = None, [g for g in (first, second) if g is not None]
        while live:
            for gen in list(live):
                try:
                    next(gen)
                except StopIteration as stop:
                    live.remove(gen)
                    if gen is first:
                        result = stop.value
        return result

    n_chunks = tb // CHUNK
    ctx = emit_interleaved(state_free(0), None)
    for c in range(n_chunks):
        nxt = state_free(c + 1) if c + 1 < n_chunks else None
        ctx_next = emit_interleaved(nxt, state_dependent(ctx))
        ctx = ctx_next

    o = o_s[...]
    mean = _head_sum(o, ones_bd) * (1.0 / HEAD_DIM)
    oc = o - mean
    var = _head_sum(oc * oc, ones_bd) * (1.0 / HEAD_DIM)
    o = oc * lax.rsqrt(var + RWKV_LN_EPS) * lng_ref[...] + lnb_ref[...]
    rr = r_s[...]
    bonus = _head_sum(rr * k_s[...] * rk_ref[...], ones_bd)
    o = o + bonus * v_s[...]
    y_ref[0] = (o * gate).astype(y_ref.dtype)


def _rwkv(cols, mu, w0, wup, a0, aup, gup, kk, ka, rk, lng, lnb):
    b, lp, _ = cols.shape
    tb = TIME_BLOCK
    par = lambda a: pl.BlockSpec(a.shape, lambda b_, j: (0, 0))
    params = (mu, w0, wup, a0, aup, gup, kk, ka, rk, lng, lnb)
    return pl.pallas_call(
        _rwkv_body,
        grid=(b, lp // tb),
        in_specs=[pl.BlockSpec((1, tb, RWKV_COLS), lambda b_, j: (b_, j, 0))] + [par(a) for a in params],
        out_specs=pl.BlockSpec((1, tb, D_MODEL), lambda b_, j: (b_, j, 0)),
        out_shape=jax.ShapeDtypeStruct((b, lp, D_MODEL), BF16),
        scratch_shapes=[pltpu.VMEM((tb + HALO, RWKV_COLS), F32)]
        + [pltpu.VMEM((tb, D_MODEL), F32) for _ in range(7)]
        + [pltpu.VMEM((D_MODEL // GROUP_W, GROUP_W, GROUP_W), F32),
           pltpu.VMEM((BD_SLOTS, GROUP_W, GROUP_W), BF16)],
        compiler_params=_cparams("parallel", "arbitrary"),
        name="rwkv7_branch",
    )(cols, *params)


def _gelu_tanh(x):
    return 0.5 * x * (1.0 + jnp.tanh(math.sqrt(2.0 / math.pi) * (x + 0.044715 * (x * x * x))))


def _rglru_body(pad, gb_ref, xr_ref, cw_ref, cb_ref, gxw_ref, gxb_ref, gaw_ref, gab_ref, lam_ref,
                y_ref, ext_ref, hc_ref):
    tb = xr_ref.shape[1]
    j = pl.program_id(1)

    @pl.when(j == 0)
    def _():
        hc_ref[...] = jnp.zeros(hc_ref.shape, F32)

    _shift_halo(ext_ref, xr_ref[0], j == 0)
    xf = jnp.broadcast_to(cb_ref[...], (tb, D_MODEL))
    for k in range(CONV_K):
        xf = xf + cw_ref[k:k + 1, :] * ext_ref[pl.ds(HALO - (CONV_K - 1) + k, tb), :]

    gx, ga = [], []
    for hblk in range(LRU_BLOCKS):
        xb = xf[:, hblk * LRU_BLOCK:(hblk + 1) * LRU_BLOCK].astype(BF16)
        gx.append(_dot(xb, gxw_ref[hblk]))
        ga.append(_dot(xb, gaw_ref[hblk]))
    gate_x = _sigmoid(jnp.concatenate(gx, axis=1) + gxb_ref[...])
    gate_a = _sigmoid(jnp.concatenate(ga, axis=1) + gab_ref[...])
    log_a = -LRU_C * gate_a * _softplus(-lam_ref[...])
    a = jnp.exp(log_a)
    u = jnp.sqrt(1.0 - jnp.exp(2.0 * log_a)) * (gate_x * xf)
    row = lax.broadcasted_iota(jnp.int32, (tb, 1), 0)
    u = jnp.where(j * tb + row >= pad, u, 0.0)

    a = a.reshape(tb // SUBLANES, SUBLANES, D_MODEL)
    u = u.reshape(tb // SUBLANES, SUBLANES, D_MODEL)
    sub = lax.broadcasted_iota(jnp.int32, (1, SUBLANES, 1), 1)
    d = 1
    while d < SUBLANES:
        keep = sub >= d
        a_sh = jnp.where(keep, pltpu.roll(a, d, 1), 1.0)
        u_sh = jnp.where(keep, pltpu.roll(u, d, 1), 0.0)
        u = u + a * u_sh
        a = a * a_sh
        d *= 2
    h = hc_ref[...]
    hs = []
    for grp in range(tb // SUBLANES):
        hs.append(a[grp] * h + u[grp])
        h = hs[-1][SUBLANES - 1:SUBLANES, :]
    hc_ref[...] = h
    y_ref[0] = (jnp.concatenate(hs, axis=0) * _gelu_tanh(gb_ref[0])).astype(y_ref.dtype)


def _rglru(proj, cw, cb, gxw, gxb, gaw, gab, lam, pad):
    b, lp, _ = proj.shape
    tb = TIME_BLOCK
    par2 = lambda a: pl.BlockSpec(a.shape, lambda b_, j: (0, 0))
    par3 = lambda a: pl.BlockSpec(a.shape, lambda b_, j: (0, 0, 0))
    return pl.pallas_call(
        functools.partial(_rglru_body, pad),
        grid=(b, lp // tb),
        in_specs=[pl.BlockSpec((1, tb, D_MODEL), lambda b_, j: (b_, j, 0)),
                  pl.BlockSpec((1, tb, D_MODEL), lambda b_, j: (b_, j, 1)),
                  par2(cw), par2(cb), par3(gxw), par2(gxb), par3(gaw), par2(gab), par2(lam)],
        out_specs=pl.BlockSpec((1, tb, D_MODEL), lambda b_, j: (b_, j, 0)),
        out_shape=jax.ShapeDtypeStruct((b, lp, D_MODEL), BF16),
        scratch_shapes=[pltpu.VMEM((tb + HALO, D_MODEL), F32), pltpu.VMEM((1, D_MODEL), F32)],
        compiler_params=_cparams("parallel", "arbitrary"),
        name="rglru_branch",
    )(proj, proj, cw, cb, gxw, gxb, gaw, gab, lam)


def _proj_ln_body(n_in, with_router, *refs):
    xs = refs[:n_in]
    ws = refs[n_in:2 * n_in]
    h_ref, g_ref, b_ref = refs[2 * n_in:2 * n_in + 3]
    rest = refs[2 * n_in + 3:]
    mix = _dot(xs[0][0], ws[0][...])
    for x_ref, w_ref in zip(xs[1:], ws[1:]):
        mix = mix + _dot(x_ref[0], w_ref[...])
    out = _layer_norm(DEEPNORM_ALPHA * h_ref[0] + mix, g_ref[...], b_ref[...])
    if not with_router:
        o_ref, ob_ref = rest
        ob_ref[0] = out.astype(BF16)
    else:
        wr_ref, o_ref, gate_ref, dest_ref, cnt_ref, base_ref = rest
        tm = out.shape[0]
        first = (pl.program_id(0) == 0) & (pl.program_id(1) == 0)

        @pl.when(first)
        def _():
            base_ref[...] = jnp.zeros(base_ref.shape, F32)

        o_hi = out.astype(BF16)
        o_lo = (out - o_hi.astype(F32)).astype(BF16)
        wr = wr_ref[...]
        w_hi = wr.astype(BF16)
        w_lo = (wr - w_hi.astype(F32)).astype(BF16)
        logits = _dot(o_hi, w_hi) + (_dot(o_lo, w_hi) + _dot(o_hi, w_lo))
        lane = lax.broadcasted_iota(jnp.int32, logits.shape, 1)
        logits = jnp.where(lane < N_EXPERTS, logits, -jnp.inf)
        m1 = jnp.max(logits, axis=-1, keepdims=True)
        i1 = jnp.min(jnp.where(logits == m1, lane, LANE), axis=-1, keepdims=True)
        rest_l = jnp.where(lane == i1, -jnp.inf, logits)
        m2 = jnp.max(rest_l, axis=-1, keepdims=True)
        i2 = jnp.min(jnp.where(rest_l == m2, lane, LANE), axis=-1, keepdims=True)
        e2 = jnp.exp(m2 - m1)
        gate_ref[0] = jnp.where(lane == 0, 1.0 / (1.0 + e2), 0.0) + jnp.where(lane == 1, e2 / (1.0 + e2), 0.0)
        hot1 = lane == i1
        hot2 = lane == i2
        both = jnp.where(hot1 | hot2, 1.0, 0.0)
        ts = tm // RANK_SPLIT
        tri_r = lax.broadcasted_iota(jnp.int32, (ts, ts), 0)
        tri_c = lax.broadcasted_iota(jnp.int32, (ts, ts), 1)
        tri = jnp.where(tri_c < tri_r, 1.0, 0.0).astype(BF16)
        base = base_ref[...]
        before = []
        for q in range(RANK_SPLIT):
            sub = both[q * ts:(q + 1) * ts]
            before.append(_dot(tri, sub.astype(BF16)) + base)
            base = base + jnp.sum(sub, axis=0, keepdims=True)
        before = jnp.concatenate(before, axis=0)
        rank1 = jnp.sum(jnp.where(hot1, before, 0.0), axis=-1, keepdims=True).astype(jnp.int32)
        rank2 = jnp.sum(jnp.where(hot2, before, 0.0), axis=-1, keepdims=True).astype(jnp.int32)
        dest_ref[0] = (jnp.where(lane == 0, i1, 0) + jnp.where(lane == 1, i2, 0)
                       + jnp.where(lane == 2, rank1, 0) + jnp.where(lane == 3, rank2, 0))
        base_ref[...] = base
        cnt_ref[...] = base.astype(jnp.int32)
    o_ref[0] = out


def _proj_ln(xs, ws, h, g, bb, tm, w_router=None):
    b, lp, _ = h.shape
    n_in = len(xs)
    row = lambda w: pl.BlockSpec((1, tm, w), lambda b_, i: (b_, i, 0))
    par = lambda a: pl.BlockSpec(a.shape, lambda b_, i: (0, 0))
    in_specs = [row(x.shape[2]) for x in xs] + [par(w) for w in ws] + [row(D_MODEL), par(g), par(bb)]
    args = list(xs) + list(ws) + [h, g, bb]
    if w_router is None:
        out_specs = [row(D_MODEL), row(D_MODEL)]
        out_shape = [jax.ShapeDtypeStruct((b, lp, D_MODEL), F32), jax.ShapeDtypeStruct((b, lp, D_MODEL), BF16)]
        scratch = []
        sem = ("parallel", "parallel")
    else:
        in_specs.append(par(w_router))
        args.append(w_router)
        out_specs = [row(D_MODEL), row(LANE), row(LANE), pl.BlockSpec((1, LANE), lambda b_, i: (0, 0))]
        out_shape = [jax.ShapeDtypeStruct((b, lp, D_MODEL), F32), jax.ShapeDtypeStruct((b, lp, LANE), F32),
                     jax.ShapeDtypeStruct((b, lp, LANE), jnp.int32), jax.ShapeDtypeStruct((1, LANE), jnp.int32)]
        scratch = [pltpu.VMEM((1, LANE), F32)]
        sem = ("arbitrary", "arbitrary")
    return pl.pallas_call(
        functools.partial(_proj_ln_body, n_in, w_router is not None),
        grid=(b, lp // tm),
        in_specs=in_specs, out_specs=out_specs, out_shape=out_shape, scratch_shapes=scratch,
        compiler_params=_cparams(*sem),
        name="out_proj_layernorm",
    )(*args)


def _ffn_body(pad, xb_ref, h_ref, wg_ref, wu_ref, wd_ref, g_ref, b_ref, o_ref, ob_ref, acc_ref):
    f = pl.program_id(2)
    tm = xb_ref.shape[1]

    @pl.when(f == 0)
    def _():
        acc_ref[...] = jnp.zeros(acc_ref.shape, F32)

    x = xb_ref[0]
    act = _silu(_dot(x, wg_ref[...])) * _dot(x, wu_ref[...])
    acc_ref[...] += _dot(act.astype(BF16), wd_ref[...])

    @pl.when(f == pl.num_programs(2) - 1)
    def _():
        out = _layer_norm(DEEPNORM_ALPHA * h_ref[0] + acc_ref[...], g_ref[...], b_ref[...])
        if pad:
            pos = pl.program_id(1) * tm + lax.broadcasted_iota(jnp.int32, (tm, 1), 0)
            out = jnp.where(pos >= pad, out, 0.0)
        o_ref[0] = out
        ob_ref[0] = out.astype(BF16)


def _ffn(xb, h, w_gu, w_down, g, bb, tm, zero_pad):
    b, lp, _ = h.shape
    nf = D_FF // FF_TILE
    tm = tm // FFN_ROW_SPLIT
    row = pl.BlockSpec((1, tm, D_MODEL), lambda b_, i, f: (b_, i, 0))
    par = lambda a: pl.BlockSpec(a.shape, lambda b_, i, f: (0, 0))
    return pl.pallas_call(
        functools.partial(_ffn_body, zero_pad),
        grid=(b, lp // tm, nf),
        in_specs=[row, row,
                  pl.BlockSpec((D_MODEL, FF_TILE), lambda b_, i, f: (0, f)),
                  pl.BlockSpec((D_MODEL, FF_TILE), lambda b_, i, f: (0, f + nf)),
                  pl.BlockSpec((FF_TILE, D_MODEL), lambda b_, i, f: (f, 0)),
                  par(g), par(bb)],
        out_specs=[row, row],
        out_shape=[jax.ShapeDtypeStruct((b, lp, D_MODEL), F32), jax.ShapeDtypeStruct((b, lp, D_MODEL), BF16)],
        scratch_shapes=[pltpu.VMEM((tm, D_MODEL), F32)],
        compiler_params=_cparams("parallel", "parallel", "arbitrary"),
        name="swiglu_layernorm",
    )(xb, h, w_gu, w_gu, w_down, g, bb)


def _row_copy(src, dst, sem):
    return pltpu.make_async_copy(src, dst, sem)


def _dispatch_body(tg, ztile_ref, nz_ref, dest_ref, x_ref, o_hbm, zbuf, sem, zsem):
    tm = x_ref.shape[1]

    @pl.when(pl.program_id(0) == 0)
    def _():
        zbuf[...] = jnp.zeros(zbuf.shape, F32)
        for phase in ("start", "wait"):
            for k in range(ztile_ref.shape[0]):
                @pl.when(k < nz_ref[0])
                def _():
                    copy = pltpu.make_async_copy(zbuf, o_hbm.at[pl.ds(ztile_ref[k] * tg, tg)], zsem)
                    copy.start() if phase == "start" else copy.wait()

    def issue(r, c):
        for k in range(2):
            _row_copy(x_ref.at[0, pl.ds(r, 1)], o_hbm.at[pl.ds(dest_ref[0, 0, 2 * r + k], 1)], sem).start()
        return c

    lax.fori_loop(0, tm, issue, 0, unroll=8)
    for k in range(2):
        _row_copy(x_ref.at[0], o_hbm.at[pl.ds(0, tm)], sem).wait()


def _dispatch(h, dest, ztile, nz, n_tiles, tg):
    b, lp, _ = h.shape
    tm = dest.shape[2] // 2
    nt = lp // tm
    grid_spec = pltpu.PrefetchScalarGridSpec(
        num_scalar_prefetch=2,
        grid=(b * nt,),
        in_specs=[pl.BlockSpec((1, 1, 2 * tm), lambda i, zt, nz_: (i, 0, 0), memory_space=pltpu.SMEM),
                  pl.BlockSpec((1, tm, D_MODEL), lambda i, zt, nz_: (i // nt, i % nt, 0))],
        out_specs=pl.BlockSpec(memory_space=pl.ANY),
        scratch_shapes=[pltpu.VMEM((tg, D_MODEL), F32), pltpu.SemaphoreType.DMA(()), pltpu.SemaphoreType.DMA(())],
    )
    return pl.pallas_call(
        functools.partial(_dispatch_body, tg),
        grid_spec=grid_spec,
        out_shape=jax.ShapeDtypeStruct((n_tiles * tg, D_MODEL), F32),
        compiler_params=_cparams("arbitrary"),
        name="moe_dispatch",
    )(ztile, nz, dest, h)


def _experts_body(exp_ref, used_ref, x_ref, wg_ref, wu_ref, wd_ref, o_ref, acc_ref):
    t = pl.program_id(0)
    f = pl.program_id(1)
    last = f == pl.num_programs(1) - 1

    @pl.when(used_ref[t] > 0)
    def _():
        @pl.when(f == 0)
        def _():
            acc_ref[...] = jnp.zeros(acc_ref.shape, F32)

        x = x_ref[...].astype(BF16)
        act = _silu(_dot(x, wg_ref[0])) * _dot(x, wu_ref[0])
        acc_ref[...] += _dot(act.astype(BF16), wd_ref[0])

        @pl.when(last)
        def _():
            o_ref[...] = acc_ref[...]

    @pl.when((used_ref[t] == 0) & last)
    def _():
        o_ref[...] = jnp.zeros(o_ref.shape, F32)


def _experts(xs, exp, used, w_gu, w_down, tg):
    tf = EXPERT_FF_TILE
    nf = D_FF // tf
    n_tiles = exp.shape[0]
    ff = lambda t, f, used_: jnp.where(used_[t] > 0, f, nf - 1)
    grid_spec = pltpu.PrefetchScalarGridSpec(
        num_scalar_prefetch=2,
        grid=(n_tiles, nf),
        in_specs=[pl.BlockSpec((tg, D_MODEL), lambda t, f, exp_, used_: (t, 0)),
                  pl.BlockSpec((1, D_MODEL, tf), lambda t, f, exp_, used_: (exp_[t], 0, ff(t, f, used_))),
                  pl.BlockSpec((1, D_MODEL, tf), lambda t, f, exp_, used_: (exp_[t], 0, ff(t, f, used_) + nf)),
                  pl.BlockSpec((1, tf, D_MODEL), lambda t, f, exp_, used_: (exp_[t], ff(t, f, used_), 0))],
        out_specs=pl.BlockSpec((tg, D_MODEL), lambda t, f, exp_, used_: (t, 0)),
        scratch_shapes=[pltpu.VMEM((tg, D_MODEL), F32)],
    )
    return pl.pallas_call(
        _experts_body,
        grid_spec=grid_spec,
        out_shape=jax.ShapeDtypeStruct(xs.shape, F32),
        compiler_params=_cparams("arbitrary", "arbitrary"),
        name="moe_experts",
    )(exp, used, xs, w_gu, w_gu, w_down)


def _combine_body(row0, nt, dest_ref, gate_ref, g_ref, b_ref, h_hbm, y_hbm, o_ref, buf_ref, h_buf, sem, h_sem):
    tc = o_ref.shape[1]
    i = pl.program_id(0)
    h_copy = pltpu.make_async_copy(h_hbm.at[i // nt, pl.ds(row0 + (i % nt) * tc, tc)], h_buf, h_sem)
    h_copy.start()

    def issue(r, c):
        for k in range(2):
            _row_copy(y_hbm.at[pl.ds(dest_ref[0, 0, 2 * r + k], 1)], buf_ref.at[k, pl.ds(r, 1)], sem).start()
        return c

    lax.fori_loop(0, tc, issue, 0, unroll=8)
    for k in range(2):
        _row_copy(y_hbm.at[pl.ds(0, tc)], buf_ref.at[k], sem).wait()
    h_copy.wait()
    gates = gate_ref[0]
    y = buf_ref[0] * gates[:, 0:1] + buf_ref[1] * gates[:, 1:2]
    o_ref[0] = _layer_norm(DEEPNORM_ALPHA * h_buf[...] + y, g_ref[...], b_ref[...])


def _combine(ys, dest, gates, h, g, bb, row0, seq):
    b = h.shape[0]
    tc = dest.shape[2] // 2
    nt = seq // tc
    par = lambda a: pl.BlockSpec(a.shape, lambda i: (0, 0))
    return pl.pallas_call(
        functools.partial(_combine_body, row0, nt),
        grid=(b * nt,),
        in_specs=[pl.BlockSpec((1, 1, 2 * tc), lambda i: (i, 0, 0), memory_space=pltpu.SMEM),
                  pl.BlockSpec((1, tc, LANE), lambda i: (i // nt, i % nt, 0)),
                  par(g), par(bb), pl.BlockSpec(memory_space=pl.ANY), pl.BlockSpec(memory_space=pl.ANY)],
        out_specs=pl.BlockSpec((1, tc, D_MODEL), lambda i: (i // nt, i % nt, 0)),
        out_shape=jax.ShapeDtypeStruct((b, seq, D_MODEL), F32),
        scratch_shapes=[pltpu.VMEM((2, tc, D_MODEL), F32), pltpu.VMEM((tc, D_MODEL), F32),
                        pltpu.SemaphoreType.DMA(()), pltpu.SemaphoreType.DMA(())],
        compiler_params=_cparams("arbitrary"),
        name="moe_combine_layernorm",
    )(dest, gates, g, bb, h, ys)


def _moe(h, gates, dest_lanes, counts, w_gu, w_down, g, bb, tm, row0):
    b, lp, _ = h.shape
    seq = lp - row0
    tc = max(t for t in range(SUBLANES, min(seq, 1024) + 1, SUBLANES) if seq % t == 0)
    tg = EXPERT_TILE
    n_tiles = -(-2 * b * lp // tg) + N_EXPERTS
    cnt = counts[0, :N_EXPERTS]
    tiles = (cnt + tg - 1) // tg
    ends = jnp.cumsum(tiles)
    group_row0 = (ends - tiles) * tg
    hot = dest_lanes[:, :, 0:2, None] == jnp.arange(N_EXPERTS, dtype=jnp.int32)
    dest = jnp.sum(jnp.where(hot, group_row0, 0), axis=-1) + dest_lanes[:, :, 2:4]
    dest_frames = dest[:, row0:].reshape(b * (seq // tc), 1, 2 * tc)
    dest = dest.reshape(b * (lp // tm), 1, 2 * tm)
    t = jnp.arange(n_tiles, dtype=jnp.int32)
    used = (t < ends[-1]).astype(jnp.int32)
    exp = jnp.minimum(jnp.searchsorted(ends, jnp.minimum(t, ends[-1] - 1), side="right"),
                      N_EXPERTS - 1).astype(jnp.int32)
    never = jnp.int32(n_tiles)
    tail = ends[-1] + jnp.arange(N_EXPERTS, dtype=jnp.int32)
    ztile = jnp.sort(jnp.concatenate([jnp.where(tiles > 0, ends - 1, never), jnp.where(tail < n_tiles, tail, never)]))
    nz = jnp.sum(ztile < never).astype(jnp.int32).reshape(1)
    xs = _dispatch(h, dest, ztile.astype(jnp.int32), nz, n_tiles, tg)
    ys = _experts(xs, exp, used, w_gu, w_down, tg)
    return _combine(ys, dest_frames, gates[:, row0:], h, g, bb, row0, seq)


def _rep(v):
    return jnp.repeat(v.astype(F32), HEAD_DIM)[None, :]


def _row(v):
    return v.astype(F32).reshape(1, -1)


def kernel(x, meta, ev_w_in, ev_conv_w, ev_conv_b, ev_dt_bias, ev_a_log, ev_d_skip, ev_ssm_norm, ev_shift_mu, ev_w0, ev_w_up, ev_a0, ev_a_up, ev_g_up, ev_k_k, ev_k_a, ev_r_k, ev_lnx_g, ev_lnx_b, ev_w_out, ev_ln1_g, ev_ln1_b, ev_ffn_w_gu, ev_ffn_w_down, ev_ln2_g, ev_ln2_b, od_w_in, od_conv_w, od_conv_b, od_gx_w, od_gx_b, od_ga_w, od_ga_b, od_lambda, od_w_out, od_ln1_g, od_ln1_b, od_router, od_exp_w_gu, od_exp_w_down, od_ln2_g, od_ln2_b):
    b, seq, d = x.shape
    assert d == D_MODEL
    l = seq + N_META
    pad = (-l) % CHUNK
    lp = l + pad
    assert lp % TIME_BLOCK == 0, lp
    tm = _row_tile(lp)

    h = jnp.concatenate([jnp.zeros((b, pad, d), x.dtype),
                         jnp.broadcast_to(meta.astype(x.dtype)[None], (b, N_META, d)), x], axis=1)
    hb = h.astype(BF16)

    i = 0
    w_in = ev_w_in[i]
    o1 = D_MODEL
    o2 = o1 + SSM_CONV_DIM
    o3 = o2 + N_HEADS
    z = _mm(hb, w_in[:, :o1].astype(BF16), F32, tm)
    xbc = _mm(hb, w_in[:, o1:o2].astype(BF16), F32, tm)
    dtx = _mm(hb, jnp.repeat(w_in[:, o2:o3], HEAD_DIM, axis=1).astype(BF16), F32, tm)
    cols = _mm(hb, w_in[:, o3:].astype(BF16), F32, tm)

    y_a = _ssd(z, xbc, dtx, ev_conv_w[i], _row(ev_conv_b[i]), _rep(ev_dt_bias[i]), _rep(ev_a_log[i]),
               _rep(ev_d_skip[i]), _row(ev_ssm_norm[i]), pad)

    zeros64 = jnp.zeros((64, D_MODEL), F32)
    wup = jnp.concatenate([ev_w_up[i], zeros64], axis=0).astype(BF16)
    aup = jnp.concatenate([zeros64, ev_a_up[i]], axis=0).astype(BF16)
    y_b = _rwkv(cols, _row(ev_shift_mu[i]), _row(ev_w0[i]), wup, _row(ev_a0[i]), aup, ev_g_up[i].astype(BF16),
                _row(ev_k_k[i]), _row(ev_k_a[i]), _row(ev_r_k[i]), _row(ev_lnx_g[i]), _row(ev_lnx_b[i]))

    w_out = ev_w_out[i].astype(BF16)
    h, hb = _proj_ln([y_a, y_b], [w_out[:D_MODEL], w_out[D_MODEL:]], h, _row(ev_ln1_g[i]), _row(ev_ln1_b[i]), tm)
    h, hb = _ffn(hb, h, ev_ffn_w_gu[i].astype(BF16), ev_ffn_w_down[i].astype(BF16),
                 _row(ev_ln2_g[i]), _row(ev_ln2_b[i]), tm, pad)

    proj = _mm(hb, od_w_in[i].astype(BF16), F32, tm)
    y_c = _rglru(proj, od_conv_w[i], _row(od_conv_b[i]), od_gx_w[i].astype(BF16), _row(od_gx_b[i]),
                 od_ga_w[i].astype(BF16), _row(od_ga_b[i]), _row(od_lambda[i]), pad)
    w_router = jnp.pad(od_router[i], ((0, 0), (0, LANE - N_EXPERTS)))
    h, gates, dest, counts = _proj_ln([y_c], [od_w_out[i].astype(BF16)], h, _row(od_ln1_g[i]), _row(od_ln1_b[i]), tm,
                                      w_router=w_router)
    return _moe(h, gates, dest, counts, od_exp_w_gu[i].astype(BF16), od_exp_w_down[i].astype(BF16),
                _row(od_ln2_g[i]), _row(od_ln2_b[i]), tm, pad + N_META)
```

```python
import functools
import math

import jax
import jax.numpy as jnp
from jax import lax
from jax.experimental import pallas as pl
from jax.experimental.pallas import tpu as pltpu

F32 = jnp.float32
BF16 = jnp.bfloat16

D_MODEL = 1024
N_META = 16
CHUNK = 64
HEAD_DIM = 64
N_HEADS = D_MODEL // HEAD_DIM
SSM_GROUPS = 2
SSM_STATE = 128
SSM_CONV_DIM = D_MODEL + 2 * SSM_GROUPS * SSM_STATE
CONV_K = 4
RWKV_LORA = 256
RWKV_COLS = 3 * D_MODEL + RWKV_LORA
RWKV_LN_EPS = 64e-5
LRU_BLOCKS = 8
LRU_BLOCK = D_MODEL // LRU_BLOCKS
LRU_C = 8.0
D_FF = 2816
N_EXPERTS = 8
DEPTH = 2
DEEPNORM_ALPHA = (2 * DEPTH) ** 0.25
LN_EPS = 1e-5

LANE = 128
SUBLANES = 8
ROW_TILE = (SUBLANES, LANE)
HALO = 8
GROUP_HEADS = 4
GROUP_W = GROUP_HEADS * HEAD_DIM
BD_SLOTS = 8
TIME_BLOCK = 192
FF_TILE = 1408
FFN_ROW_SPLIT = 2
RANK_SPLIT = 4
EXPERT_TILE = 512
EXPERT_FF_TILE = FF_TILE
VMEM_LIMIT = 56 * 1024 * 1024


def _cparams(*sem):
    return pltpu.CompilerParams(dimension_semantics=sem, vmem_limit_bytes=VMEM_LIMIT)


def _row_tile(lp):
    best = 16
    for t in range(16, min(lp, 1376) + 1, 16):
        if lp % t == 0:
            best = t
    return best


def _col_tile(n):
    for t in (512, 1664, 256, 128):
        if n % t == 0:
            return t
    raise ValueError(n)


def _sigmoid(x):
    return 1.0 / (1.0 + jnp.exp(-x))


def _silu(x):
    return x * _sigmoid(x)


def _softplus(x):
    return jnp.maximum(x, 0.0) + jnp.log(1.0 + jnp.exp(-jnp.abs(x)))


def _dot(a, b):
    return jnp.dot(a, b, preferred_element_type=F32)


def _dot_nt(a, b):
    return lax.dot_general(a, b, (((1,), (1,)), ((), ())), preferred_element_type=F32)


def _dot_tn(a, b):
    return lax.dot_general(a, b, (((0,), (0,)), ((), ())), preferred_element_type=F32)


def _prefix_sum(tri, x):
    hi = x.astype(BF16)
    r1 = x - hi.astype(F32)
    mid = r1.astype(BF16)
    lo = (r1 - mid.astype(F32)).astype(BF16)
    return _dot(tri, hi) + (_dot(tri, mid) + _dot(tri, lo))


def _layer_norm(x, g, b):
    mu = jnp.mean(x, axis=-1, keepdims=True)
    xc = x - mu
    var = jnp.mean(xc * xc, axis=-1, keepdims=True)
    return xc * lax.rsqrt(var + LN_EPS) * g + b


def _mm_body(x_ref, w_ref, o_ref):
    o_ref[0] = _dot(x_ref[0], w_ref[...]).astype(o_ref.dtype)


def _mm(x, w, out_dtype, tm):
    b, lp, k = x.shape
    n = w.shape[1]
    tn = _col_tile(n)
    return pl.pallas_call(
        _mm_body,
        grid=(b, lp // tm, n // tn),
        in_specs=[pl.BlockSpec((1, tm, k), lambda b_, i, j: (b_, i, 0)),
                  pl.BlockSpec((k, tn), lambda b_, i, j: (0, j))],
        out_specs=pl.BlockSpec((1, tm, tn), lambda b_, i, j: (b_, i, j)),
        out_shape=jax.ShapeDtypeStruct((b, lp, n), out_dtype),
        compiler_params=_cparams("parallel", "parallel", "arbitrary"),
        name="dense_matmul",
    )(x, w)


def _chunk_consts():
    row = lax.broadcasted_iota(jnp.int32, (CHUNK, GROUP_W), 0)
    col = lax.broadcasted_iota(jnp.int32, (CHUNK, GROUP_W), 1) % HEAD_DIM
    r2 = lax.broadcasted_iota(jnp.int32, (GROUP_W, GROUP_W), 0) // HEAD_DIM
    c2 = lax.broadcasted_iota(jnp.int32, (GROUP_W, GROUP_W), 1) // HEAD_DIM
    tri_r = lax.broadcasted_iota(jnp.int32, (CHUNK, CHUNK), 0)
    tri_c = lax.broadcasted_iota(jnp.int32, (CHUNK, CHUNK), 1)
    tri = jnp.where(tri_c <= tri_r, 1.0, 0.0).astype(BF16)
    return row, col, (r2 == c2), tri


def _block_diag(y, bd_mask):
    return jnp.where(bd_mask, jnp.concatenate([y] * GROUP_HEADS, axis=0), jnp.zeros((), y.dtype))


def _shift_halo(ext_ref, blk, first):
    tb = blk.shape[0]

    @pl.when(first)
    def _():
        ext_ref[0:HALO, :] = jnp.zeros((HALO, ext_ref.shape[1]), F32)

    @pl.when(jnp.logical_not(first))
    def _():
        ext_ref[0:HALO, :] = ext_ref[tb:tb + HALO, :]

    ext_ref[HALO:HALO + tb, :] = blk


def _ssd_body(pad, z_ref, xbc_ref, dtx_ref, cw_ref, cb_ref, dtb_ref, alog_ref, dsk_ref, nw_ref,
              y_ref, ext_ref, xc_ref, dt_ref, yb_ref, h_ref):
    tb = xbc_ref.shape[1]
    j = pl.program_id(1)

    @pl.when(j == 0)
    def _():
        h_ref[...] = jnp.zeros(h_ref.shape, F32)

    _shift_halo(ext_ref, xbc_ref[0], j == 0)
    acc = jnp.broadcast_to(cb_ref[...], (tb, SSM_CONV_DIM))
    for k in range(CONV_K):
        acc = acc + cw_ref[k:k + 1, :] * ext_ref[pl.ds(HALO - (CONV_K - 1) + k, tb), :]
    xc_ref[...] = _silu(acc)

    pos = j * tb + lax.broadcasted_iota(jnp.int32, (tb, 1), 0)
    dt_ref[...] = jnp.where(pos >= pad, _softplus(dtx_ref[0] + dtb_ref[...]), 0.0)

    row, col, bd_mask, tri = _chunk_consts()
    row_w = jnp.concatenate([row] * (D_MODEL // GROUP_W), axis=1)
    col_w = jnp.concatenate([col] * (D_MODEL // GROUP_W), axis=1)
    a_neg = -jnp.exp(alog_ref[...])
    gw = D_MODEL // SSM_GROUPS

    def chunk(c, carry):
        rows = pl.ds(pl.multiple_of(c * CHUNK, CHUNK), CHUNK)
        xs = xc_ref[rows, 0:D_MODEL]
        dt = dt_ref[rows, :]
        xdt = xs * dt
        acs = _prefix_sum(tri, dt * a_neg)
        acs_t = jnp.sum(jnp.where(row_w == col_w, acs, 0.0), axis=0, keepdims=True)
        decay = jnp.where(row_w >= col_w, jnp.exp(jnp.minimum(acs - acs_t, 0.0)), 0.0)
        e_acs = jnp.exp(acs)
        last = acs[CHUNK - 1:CHUNK, :]
        xend = xdt * jnp.exp(last - acs)
        e_last = jnp.exp(last)
        groups = range(SSM_GROUPS)
        gsl = [slice(g * gw, (g + 1) * gw) for g in groups]
        bg = [xc_ref[rows, D_MODEL + g * SSM_STATE:D_MODEL + (g + 1) * SSM_STATE].astype(BF16) for g in groups]
        cg = [xc_ref[rows, D_MODEL + (SSM_GROUPS + g) * SSM_STATE:
                     D_MODEL + (SSM_GROUPS + g + 1) * SSM_STATE].astype(BF16) for g in groups]
        scores = [_dot_nt(cg[g], jnp.concatenate([bg[g]] * GROUP_HEADS, axis=0)) for g in groups]
        hg = [h_ref[:, gsl[g]] for g in groups]
        y_off = [_dot(cg[g], hg[g].astype(BF16)) * e_acs[:, gsl[g]] for g in groups]
        st = [_dot_tn(bg[g], xend[:, gsl[g]].astype(BF16)) for g in groups]
        for g in groups:
            h_ref[:, gsl[g]] = hg[g] * e_last[:, gsl[g]] + st[g]
        for g in groups:
            for t in range(gw // GROUP_W):
                lo = g * gw + t * GROUP_W
                m = (scores[g] * decay[:, lo:lo + GROUP_W]).astype(BF16)
                xbd = _block_diag(xdt[:, lo:lo + GROUP_W].astype(BF16), bd_mask)
                yb_ref[rows, lo:lo + GROUP_W] = (_dot(m, xbd) + y_off[g][:, t * GROUP_W:(t + 1) * GROUP_W])
        return carry

    lax.fori_loop(0, tb // CHUNK, chunk, 0)

    y = (yb_ref[...] + dsk_ref[...] * xc_ref[:, 0:D_MODEL]) * _silu(z_ref[0])
    outs = []
    for g in range(SSM_GROUPS):
        yg = y[:, g * gw:(g + 1) * gw]
        ms = jnp.mean(yg * yg, axis=-1, keepdims=True)
        outs.append(yg * lax.rsqrt(ms + LN_EPS) * nw_ref[:, g * gw:(g + 1) * gw])
    y_ref[0] = jnp.concatenate(outs, axis=1).astype(y_ref.dtype)


def _ssd(z, xbc, dtx, cw, cb, dtb, alog, dsk, nw, pad):
    b, lp, _ = z.shape
    tb = TIME_BLOCK
    blk = lambda w: pl.BlockSpec((1, tb, w), lambda b_, j: (b_, j, 0))
    par = lambda a: pl.BlockSpec(a.shape, lambda b_, j: (0, 0))
    return pl.pallas_call(
        functools.partial(_ssd_body, pad),
        grid=(b, lp // tb),
        in_specs=[blk(D_MODEL), blk(SSM_CONV_DIM), blk(D_MODEL)] + [par(a) for a in (cw, cb, dtb, alog, dsk, nw)],
        out_specs=blk(D_MODEL),
        out_shape=jax.ShapeDtypeStruct((b, lp, D_MODEL), BF16),
        scratch_shapes=[pltpu.VMEM((tb + HALO, SSM_CONV_DIM), F32),
                        pltpu.VMEM((tb, SSM_CONV_DIM), F32),
                        pltpu.VMEM((tb, D_MODEL), F32),
                        pltpu.VMEM((tb, D_MODEL), F32),
                        pltpu.VMEM((SSM_STATE, D_MODEL), F32)],
        compiler_params=_cparams("parallel", "arbitrary"),
        name="ssd_branch",
    )(z, xbc, dtx, cw, cb, dtb, alog, dsk, nw)


def _head_sum(x, ones_bd):
    outs = []
    for i in range(D_MODEL // LANE):
        xi = x[:, i * LANE:(i + 1) * LANE]
        hi = xi.astype(BF16)
        lo = (xi - hi.astype(F32)).astype(BF16)
        outs.append(_dot(hi, ones_bd) + _dot(lo, ones_bd))
    return jnp.concatenate(outs, axis=1)


def _rwkv_body(cols_ref, mu_ref, w0_ref, wup_ref, a0_ref, aup_ref, gup_ref, kk_ref, ka_ref, rk_ref,
               lng_ref, lnb_ref, y_ref, ext_ref, r_s, k_s, v_s, kk_s, akk_s, lw_s, o_s, s_ref, bd_s):
    tb = cols_ref.shape[1]
    j = pl.program_id(1)

    @pl.when(j == 0)
    def _():
        s_ref[...] = jnp.zeros(s_ref.shape, F32)
        bd_s[...] = jnp.zeros(bd_s.shape, BF16)

    _shift_halo(ext_ref, cols_ref[0], j == 0)
    cur = ext_ref[HALO:HALO + tb, :]
    prev = ext_ref[pl.ds(HALO - 1, tb), :]
    mixed = cur + (prev - cur) * mu_ref[...]

    lane2 = lax.broadcasted_iota(jnp.int32, (LANE, LANE), 1) // HEAD_DIM
    row2 = lax.broadcasted_iota(jnp.int32, (LANE, LANE), 0) // HEAD_DIM
    ones_bd = (lane2 == row2).astype(BF16)

    r = mixed[:, 0:D_MODEL]
    k = mixed[:, D_MODEL:2 * D_MODEL]
    v = mixed[:, 2 * D_MODEL:3 * D_MODEL]
    lora_wa = mixed[:, 3 * D_MODEL:3 * D_MODEL + LANE]
    lora_g = mixed[:, 3 * D_MODEL + LANE:3 * D_MODEL + 2 * LANE]
    zw = w0_ref[...] + _dot(jnp.tanh(lora_wa).astype(BF16), wup_ref[...])
    a = _sigmoid(a0_ref[...] + _dot(lora_wa.astype(BF16), aup_ref[...]))
    gate = _dot(_sigmoid(lora_g).astype(BF16), gup_ref[...])
    kk = k * kk_ref[...]
    kk = kk * lax.rsqrt(jnp.maximum(_head_sum(kk * kk, ones_bd), 1e-24))
    kmod = k * (1.0 + (a - 1.0) * ka_ref[...])
    r_s[...] = r
    k_s[...] = kmod
    v_s[...] = v
    kk_s[...] = kk
    akk_s[...] = a * kk
    lw_s[...] = -math.exp(-0.5) * _sigmoid(zw)

    row, col, bd_mask, tri = _chunk_consts()
    eye = row == col
    strict = row > col
    incl = row >= col
    lvl_masks = []
    s = 2
    while s < CHUNK:
        lvl_masks.append((row // (2 * s) == col // (2 * s)) & ((row // s) % 2 == 1) & ((col // s) % 2 == 0))
        s *= 2
    pair = (row == col + 1) & (row % 2 == 1)

    bd_slot = [0]

    def block_diag(y):
        return _block_diag(y, bd_mask)

    def bdmm(x, y):
        return _dot(x.astype(BF16), block_diag(y.astype(BF16)))

    groups = range(D_MODEL // GROUP_W)
    sls = [slice(g * GROUP_W, (g + 1) * GROUP_W) for g in groups]

    def state_free(c):
        rows = slice(c * CHUNK, (c + 1) * CHUNK)
        lw = lw_s[rows, :]
        lc = _prefix_sum(tri, lw)
        e_p = jnp.exp(lc)
        e_m = jnp.exp(-lc)
        kkc = kk_s[rows, :]
        at = jnp.exp(lc - lw) * kkc
        bt = -(akk_s[rows, :] * e_m)
        kt = k_s[rows, :] * e_m
        rt = r_s[rows, :] * e_p
        p_last = e_p[CHUNK - 1:CHUNK, :]
        bp = bt * p_last
        kp = kt * p_last
        vc = v_s[rows, :]
        v_g = [vc[:, sl].astype(BF16) for sl in sls]
        lhs = [jnp.concatenate([at[:, sl].astype(BF16), rt[:, sl].astype(BF16)], axis=0) for sl in sls]
        g_b = [_dot_nt(lhs[g], block_diag(bt[:, sls[g]].astype(BF16))) for g in groups]
        g_k = [_dot_nt(lhs[g], block_diag(kt[:, sls[g]].astype(BF16))) for g in groups]
        a_ab = [jnp.where(strict, g_b[g][0:CHUNK], 0.0) for g in groups]
        a_rb = [jnp.where(incl, g_b[g][CHUNK:], 0.0) for g in groups]
        a_k = [jnp.where(jnp.concatenate([strict, incl], axis=0), g_k[g], 0.0) for g in groups]
        a_v = [bdmm(a_k[g], v_g[g]) for g in groups]
        yield
        inv = [jnp.where(eye, 1.0, 0.0) + jnp.where(pair, a_ab[g], 0.0) for g in groups]
        for m in lvl_masks:
            low = [bdmm(inv[g], jnp.where(m, a_ab[g], 0.0)) for g in groups]
            yield
            inv = [inv[g] + bdmm(low[g], inv[g]) for g in groups]
            yield
        bkp = [jnp.concatenate([bp[:, sl].astype(BF16), kp[:, sl].astype(BF16)], axis=0) for sl in sls]
        return dict(rows=rows, lhs=lhs, v_g=v_g, a_rb=a_rb, a_v=a_v, inv=inv, bkp=bkp, p_last=p_last)

    def state_dependent(ctx):
        s_q = [_dot_nt(ctx["lhs"][g], s_ref[g].astype(BF16)) for g in groups]
        yield
        u = [bdmm(ctx["inv"][g], s_q[g][0:CHUNK] + ctx["a_v"][g][0:CHUNK]) for g in groups]
        yield
        y = [s_q[g][CHUNK:] + ctx["a_v"][g][CHUNK:] + bdmm(ctx["a_rb"][g], u[g]) for g in groups]
        upd = [_dot_tn(jnp.concatenate([u[g].astype(BF16), ctx["v_g"][g]], axis=0), ctx["bkp"][g]) for g in groups]
        for g in groups:
            o_s[ctx["rows"], sls[g]] = y[g]
            s_ref[g] = s_ref[g] * ctx["p_last"][:, sls[g]] + jnp.where(bd_mask, upd[g], 0.0)

    def emit_interleaved(first, second):
        result, live = None, [g for g in (first, second) if g is not None]
        while live:
            for gen in list(live):
                try:
                    next(gen)
                except StopIteration as stop:
                    live.remove(gen)
                    if gen is first:
                        result = stop.value
        return result

    n_chunks = tb // CHUNK
    ctx = emit_interleaved(state_free(0), None)
    for c in range(n_chunks):
        nxt = state_free(c + 1) if c + 1 < n_chunks else None
        ctx_next = emit_interleaved(nxt, state_dependent(ctx))
        ctx = ctx_next

    o = o_s[...]
    mean = _head_sum(o, ones_bd) * (1.0 / HEAD_DIM)
    oc = o - mean
    var = _head_sum(oc * oc, ones_bd) * (1.0 / HEAD_DIM)
    o = oc * lax.rsqrt(var + RWKV_LN_EPS) * lng_ref[...] + lnb_ref[...]
    rr = r_s[...]
    bonus = _head_sum(rr * k_s[...] * rk_ref[...], ones_bd)
    o = o + bonus * v_s[...]
    y_ref[0] = (o * gate).astype(y_ref.dtype)


def _rwkv(cols, mu, w0, wup, a0, aup, gup, kk, ka, rk, lng, lnb):
    b, lp, _ = cols.shape
    tb = TIME_BLOCK
    par = lambda a: pl.BlockSpec(a.shape, lambda b_, j: (0, 0))
    params = (mu, w0, wup, a0, aup, gup, kk, ka, rk, lng, lnb)
    return pl.pallas_call(
        _rwkv_body,
        grid=(b, lp // tb),
        in_specs=[pl.BlockSpec((1, tb, RWKV_COLS), lambda b_, j: (b_, j, 0))] + [par(a) for a in params],
        out_specs=pl.BlockSpec((1, tb, D_MODEL), lambda b_, j: (b_, j, 0)),
        out_shape=jax.ShapeDtypeStruct((b, lp, D_MODEL), BF16),
        scratch_shapes=[pltpu.VMEM((tb + HALO, RWKV_COLS), F32)]
        + [pltpu.VMEM((tb, D_MODEL), F32) for _ in range(7)]
        + [pltpu.VMEM((D_MODEL // GROUP_W, GROUP_W, GROUP_W), F32),
           pltpu.VMEM((BD_SLOTS, GROUP_W, GROUP_W), BF16)],
        compiler_params=_cparams("parallel", "arbitrary"),
        name="rwkv7_branch",
    )(cols, *params)


def _gelu_tanh(x):
    return 0.5 * x * (1.0 + jnp.tanh(math.sqrt(2.0 / math.pi) * (x + 0.044715 * (x * x * x))))


def _rglru_body(pad, gb_ref, xr_ref, cw_ref, cb_ref, gxw_ref, gxb_ref, gaw_ref, gab_ref, lam_ref,
                y_ref, ext_ref, hc_ref):
    tb = xr_ref.shape[1]
    j = pl.program_id(1)

    @pl.when(j == 0)
    def _():
        hc_ref[...] = jnp.zeros(hc_ref.shape, F32)

    _shift_halo(ext_ref, xr_ref[0], j == 0)
    xf = jnp.broadcast_to(cb_ref[...], (tb, D_MODEL))
    for k in range(CONV_K):
        xf = xf + cw_ref[k:k + 1, :] * ext_ref[pl.ds(HALO - (CONV_K - 1) + k, tb), :]

    gx, ga = [], []
    for hblk in range(LRU_BLOCKS):
        xb = xf[:, hblk * LRU_BLOCK:(hblk + 1) * LRU_BLOCK].astype(BF16)
        gx.append(_dot(xb, gxw_ref[hblk]))
        ga.append(_dot(xb, gaw_ref[hblk]))
    gate_x = _sigmoid(jnp.concatenate(gx, axis=1) + gxb_ref[...])
    gate_a = _sigmoid(jnp.concatenate(ga, axis=1) + gab_ref[...])
    log_a = -LRU_C * gate_a * _softplus(-lam_ref[...])
    a = jnp.exp(log_a)
    u = jnp.sqrt(1.0 - jnp.exp(2.0 * log_a)) * (gate_x * xf)
    row = lax.broadcasted_iota(jnp.int32, (tb, 1), 0)
    u = jnp.where(j * tb + row >= pad, u, 0.0)

    a = a.reshape(tb // SUBLANES, SUBLANES, D_MODEL)
    u = u.reshape(tb // SUBLANES, SUBLANES, D_MODEL)
    sub = lax.broadcasted_iota(jnp.int32, (1, SUBLANES, 1), 1)
    d = 1
    while d < SUBLANES:
        keep = sub >= d
        a_sh = jnp.where(keep, pltpu.roll(a, d, 1), 1.0)
        u_sh = jnp.where(keep, pltpu.roll(u, d, 1), 0.0)
        u = u + a * u_sh
        a = a * a_sh
        d *= 2
    h = hc_ref[...]
    hs = []
    for grp in range(tb // SUBLANES):
        hs.append(a[grp] * h + u[grp])
        h = hs[-1][SUBLANES - 1:SUBLANES, :]
    hc_ref[...] = h
    y_ref[0] = (jnp.concatenate(hs, axis=0) * _gelu_tanh(gb_ref[0])).astype(y_ref.dtype)


def _rglru(proj, cw, cb, gxw, gxb, gaw, gab, lam, pad):
    b, lp, _ = proj.shape
    tb = TIME_BLOCK
    par2 = lambda a: pl.BlockSpec(a.shape, lambda b_, j: (0, 0))
    par3 = lambda a: pl.BlockSpec(a.shape, lambda b_, j: (0, 0, 0))
    return pl.pallas_call(
        functools.partial(_rglru_body, pad),
        grid=(b, lp // tb),
        in_specs=[pl.BlockSpec((1, tb, D_MODEL), lambda b_, j: (b_, j, 0)),
                  pl.BlockSpec((1, tb, D_MODEL), lambda b_, j: (b_, j, 1)),
                  par2(cw), par2(cb), par3(gxw), par2(gxb), par3(gaw), par2(gab), par2(lam)],
        out_specs=pl.BlockSpec((1, tb, D_MODEL), lambda b_, j: (b_, j, 0)),
        out_shape=jax.ShapeDtypeStruct((b, lp, D_MODEL), BF16),
        scratch_shapes=[pltpu.VMEM((tb + HALO, D_MODEL), F32), pltpu.VMEM((1, D_MODEL), F32)],
        compiler_params=_cparams("parallel", "arbitrary"),
        name="rglru_branch",
    )(proj, proj, cw, cb, gxw, gxb, gaw, gab, lam)


def _proj_ln_body(n_in, with_router, *refs):
    xs = refs[:n_in]
    ws = refs[n_in:2 * n_in]
    h_ref, g_ref, b_ref = refs[2 * n_in:2 * n_in + 3]
    rest = refs[2 * n_in + 3:]
    mix = _dot(xs[0][0], ws[0][...])
    for x_ref, w_ref in zip(xs[1:], ws[1:]):
        mix = mix + _dot(x_ref[0], w_ref[...])
    out = _layer_norm(DEEPNORM_ALPHA * h_ref[0] + mix, g_ref[...], b_ref[...])
    if not with_router:
        o_ref, ob_ref = rest
        ob_ref[0] = out.astype(BF16)
    else:
        wr_ref, o_ref, gate_ref, dest_ref, cnt_ref, base_ref = rest
        tm = out.shape[0]
        first = (pl.program_id(0) == 0) & (pl.program_id(1) == 0)

        @pl.when(first)
        def _():
            base_ref[...] = jnp.zeros(base_ref.shape, F32)

        o_hi = out.astype(BF16)
        o_lo = (out - o_hi.astype(F32)).astype(BF16)
        wr = wr_ref[...]
        w_hi = wr.astype(BF16)
        w_lo = (wr - w_hi.astype(F32)).astype(BF16)
        logits = _dot(o_hi, w_hi) + (_dot(o_lo, w_hi) + _dot(o_hi, w_lo))
        lane = lax.broadcasted_iota(jnp.int32, logits.shape, 1)
        logits = jnp.where(lane < N_EXPERTS, logits, -jnp.inf)
        m1 = jnp.max(logits, axis=-1, keepdims=True)
        i1 = jnp.min(jnp.where(logits == m1, lane, LANE), axis=-1, keepdims=True)
        rest_l = jnp.where(lane == i1, -jnp.inf, logits)
        m2 = jnp.max(rest_l, axis=-1, keepdims=True)
        i2 = jnp.min(jnp.where(rest_l == m2, lane, LANE), axis=-1, keepdims=True)
        e2 = jnp.exp(m2 - m1)
        gate_ref[0] = jnp.where(lane == 0, 1.0 / (1.0 + e2), 0.0) + jnp.where(lane == 1, e2 / (1.0 + e2), 0.0)
        hot1 = lane == i1
        hot2 = lane == i2
        both = jnp.where(hot1 | hot2, 1.0, 0.0)
        ts = tm // RANK_SPLIT
        tri_r = lax.broadcasted_iota(jnp.int32, (ts, ts), 0)
        tri_c = lax.broadcasted_iota(jnp.int32, (ts, ts), 1)
        tri = jnp.where(tri_c < tri_r, 1.0, 0.0).astype(BF16)
        base = base_ref[...]
        before = []
        for q in range(RANK_SPLIT):
            sub = both[q * ts:(q + 1) * ts]
            before.append(_dot(tri, sub.astype(BF16)) + base)
            base = base + jnp.sum(sub, axis=0, keepdims=True)
        before = jnp.concatenate(before, axis=0)
        rank1 = jnp.sum(jnp.where(hot1, before, 0.0), axis=-1, keepdims=True).astype(jnp.int32)
        rank2 = jnp.sum(jnp.where(hot2, before, 0.0), axis=-1, keepdims=True).astype(jnp.int32)
        dest_ref[0] = (jnp.where(lane == 0, i1, 0) + jnp.where(lane == 1, i2, 0)
                       + jnp.where(lane == 2, rank1, 0) + jnp.where(lane == 3, rank2, 0))
        base_ref[...] = base
        cnt_ref[...] = base.astype(jnp.int32)
    o_ref[0] = out


def _proj_ln(xs, ws, h, g, bb, tm, w_router=None):
    b, lp, _ = h.shape
    n_in = len(xs)
    row = lambda w: pl.BlockSpec((1, tm, w), lambda b_, i: (b_, i, 0))
    par = lambda a: pl.BlockSpec(a.shape, lambda b_, i: (0, 0))
    in_specs = [row(x.shape[2]) for x in xs] + [par(w) for w in ws] + [row(D_MODEL), par(g), par(bb)]
    args = list(xs) + list(ws) + [h, g, bb]
    if w_router is None:
        out_specs = [row(D_MODEL), row(D_MODEL)]
        out_shape = [jax.ShapeDtypeStruct((b, lp, D_MODEL), F32), jax.ShapeDtypeStruct((b, lp, D_MODEL), BF16)]
        scratch = []
        sem = ("parallel", "parallel")
    else:
        in_specs.append(par(w_router))
        args.append(w_router)
        out_specs = [row(D_MODEL), row(LANE), row(LANE), pl.BlockSpec((1, LANE), lambda b_, i: (0, 0))]
        out_shape = [jax.ShapeDtypeStruct((b, lp, D_MODEL), F32), jax.ShapeDtypeStruct((b, lp, LANE), F32),
                     jax.ShapeDtypeStruct((b, lp, LANE), jnp.int32), jax.ShapeDtypeStruct((1, LANE), jnp.int32)]
        scratch = [pltpu.VMEM((1, LANE), F32)]
        sem = ("arbitrary", "arbitrary")
    return pl.pallas_call(
        functools.partial(_proj_ln_body, n_in, w_router is not None),
        grid=(b, lp // tm),
        in_specs=in_specs, out_specs=out_specs, out_shape=out_shape, scratch_shapes=scratch,
        compiler_params=_cparams(*sem),
        name="out_proj_layernorm",
    )(*args)


def _ffn_body(pad, xb_ref, h_ref, wg_ref, wu_ref, wd_ref, g_ref, b_ref, o_ref, ob_ref, acc_ref):
    f = pl.program_id(2)
    tm = xb_ref.shape[1]

    @pl.when(f == 0)
    def _():
        acc_ref[...] = jnp.zeros(acc_ref.shape, F32)

    x = xb_ref[0]
    act = _silu(_dot(x, wg_ref[...])) * _dot(x, wu_ref[...])
    acc_ref[...] += _dot(act.astype(BF16), wd_ref[...])

    @pl.when(f == pl.num_programs(2) - 1)
    def _():
        out = _layer_norm(DEEPNORM_ALPHA * h_ref[0] + acc_ref[...], g_ref[...], b_ref[...])
        if pad:
            pos = pl.program_id(1) * tm + lax.broadcasted_iota(jnp.int32, (tm, 1), 0)
            out = jnp.where(pos >= pad, out, 0.0)
        o_ref[0] = out
        ob_ref[0] = out.astype(BF16)


def _ffn(xb, h, w_gu, w_down, g, bb, tm, zero_pad):
    b, lp, _ = h.shape
    nf = D_FF // FF_TILE
    tm = tm // FFN_ROW_SPLIT
    row = pl.BlockSpec((1, tm, D_MODEL), lambda b_, i, f: (b_, i, 0))
    par = lambda a: pl.BlockSpec(a.shape, lambda b_, i, f: (0, 0))
    return pl.pallas_call(
        functools.partial(_ffn_body, zero_pad),
        grid=(b, lp // tm, nf),
        in_specs=[row, row,
                  pl.BlockSpec((D_MODEL, FF_TILE), lambda b_, i, f: (0, f)),
                  pl.BlockSpec((D_MODEL, FF_TILE), lambda b_, i, f: (0, f + nf)),
                  pl.BlockSpec((FF_TILE, D_MODEL), lambda b_, i, f: (f, 0)),
                  par(g), par(bb)],
        out_specs=[row, row],
        out_shape=[jax.ShapeDtypeStruct((b, lp, D_MODEL), F32), jax.ShapeDtypeStruct((b, lp, D_MODEL), BF16)],
        scratch_shapes=[pltpu.VMEM((tm, D_MODEL), F32)],
        compiler_params=_cparams("parallel", "parallel", "arbitrary"),
        name="swiglu_layernorm",
    )(xb, h, w_gu, w_gu, w_down, g, bb)


def _row_copy(src, dst, sem):
    return pltpu.make_async_copy(src, dst, sem)


def _dispatch_body(tg, ztile_ref, nz_ref, dest_ref, x_ref, o_hbm, zbuf, xrow, sem, zsem):
    tm = x_ref.shape[1]

    @pl.when(pl.program_id(0) == 0)
    def _():
        zbuf[...] = jnp.zeros(zbuf.shape, F32)
        for phase in ("start", "wait"):
            for k in range(ztile_ref.shape[0]):
                @pl.when(k < nz_ref[0])
                def _():
                    copy = pltpu.make_async_copy(zbuf, o_hbm.at[pl.ds(ztile_ref[k] * tg, tg)], zsem)
                    copy.start() if phase == "start" else copy.wait()

    xrow[...] = x_ref[0].reshape(tm, SUBLANES, LANE)

    def issue(r, c):
        for k in range(2):
            _row_copy(xrow.at[pl.ds(r, 1)], o_hbm.at[pl.ds(dest_ref[0, 0, 2 * r + k], 1)], sem).start(priority=k)
        return c

    lax.fori_loop(0, tm, issue, 0, unroll=8)
    for k in range(2):
        _row_copy(xrow, o_hbm.at[pl.ds(0, tm)], sem).wait()


def _dispatch(h, dest, ztile, nz, n_tiles, tg):
    b, lp, _ = h.shape
    tm = dest.shape[2] // 2
    nt = lp // tm
    grid_spec = pltpu.PrefetchScalarGridSpec(
        num_scalar_prefetch=2,
        grid=(b * nt,),
        in_specs=[pl.BlockSpec((1, 1, 2 * tm), lambda i, zt, nz_: (i, 0, 0), memory_space=pltpu.SMEM),
                  pl.BlockSpec((1, tm, D_MODEL), lambda i, zt, nz_: (i // nt, i % nt, 0))],
        out_specs=pl.BlockSpec(memory_space=pl.ANY),
        scratch_shapes=[pltpu.VMEM((tg,) + ROW_TILE, F32), pltpu.VMEM((tm,) + ROW_TILE, F32),
                        pltpu.SemaphoreType.DMA(()), pltpu.SemaphoreType.DMA(())],
    )
    return pl.pallas_call(
        functools.partial(_dispatch_body, tg),
        grid_spec=grid_spec,
        out_shape=jax.ShapeDtypeStruct((n_tiles * tg,) + ROW_TILE, F32),
        compiler_params=_cparams("arbitrary"),
        name="moe_dispatch",
    )(ztile, nz, dest, h)


def _experts_body(exp_ref, used_ref, x_ref, wg_ref, wu_ref, wd_ref, o_ref, acc_ref):
    t = pl.program_id(0)
    f = pl.program_id(1)
    last = f == pl.num_programs(1) - 1

    @pl.when(used_ref[t] > 0)
    def _():
        @pl.when(f == 0)
        def _():
            acc_ref[...] = jnp.zeros(acc_ref.shape, F32)

        x = x_ref[...].reshape(x_ref.shape[0], D_MODEL).astype(BF16)
        act = _silu(_dot(x, wg_ref[0])) * _dot(x, wu_ref[0])
        acc_ref[...] += _dot(act.astype(BF16), wd_ref[0])

        @pl.when(last)
        def _():
            o_ref[...] = acc_ref[...].reshape(o_ref.shape)

    @pl.when((used_ref[t] == 0) & last)
    def _():
        o_ref[...] = jnp.zeros(o_ref.shape, F32)


def _experts(xs, exp, used, w_gu, w_down, tg):
    tf = EXPERT_FF_TILE
    nf = D_FF // tf
    n_tiles = exp.shape[0]
    ff = lambda t, f, used_: jnp.where(used_[t] > 0, f, nf - 1)
    grid_spec = pltpu.PrefetchScalarGridSpec(
        num_scalar_prefetch=2,
        grid=(n_tiles, nf),
        in_specs=[pl.BlockSpec((tg,) + ROW_TILE, lambda t, f, exp_, used_: (t, 0, 0)),
                  pl.BlockSpec((1, D_MODEL, tf), lambda t, f, exp_, used_: (exp_[t], 0, ff(t, f, used_))),
                  pl.BlockSpec((1, D_MODEL, tf), lambda t, f, exp_, used_: (exp_[t], 0, ff(t, f, used_) + nf)),
                  pl.BlockSpec((1, tf, D_MODEL), lambda t, f, exp_, used_: (exp_[t], ff(t, f, used_), 0))],
        out_specs=pl.BlockSpec((tg,) + ROW_TILE, lambda t, f, exp_, used_: (t, 0, 0)),
        scratch_shapes=[pltpu.VMEM((tg, D_MODEL), F32)],
    )
    return pl.pallas_call(
        _experts_body,
        grid_spec=grid_spec,
        out_shape=jax.ShapeDtypeStruct(xs.shape, F32),
        compiler_params=_cparams("arbitrary", "arbitrary"),
        name="moe_experts",
    )(exp, used, xs, w_gu, w_gu, w_down)


def _combine_body(row0, nt, dest_ref, gate_ref, g_ref, b_ref, h_hbm, y_hbm, o_ref, buf_ref, h_buf, sem, h_sem):
    tc = o_ref.shape[1]
    i = pl.program_id(0)
    h_copy = pltpu.make_async_copy(h_hbm.at[i // nt, pl.ds(row0 + (i % nt) * tc, tc)], h_buf, h_sem)
    h_copy.start()

    def issue(r, c):
        for k in range(2):
            _row_copy(y_hbm.at[pl.ds(dest_ref[0, 0, 2 * r + k], 1)], buf_ref.at[k, pl.ds(r, 1)], sem).start(priority=k)
        return c

    lax.fori_loop(0, tc, issue, 0, unroll=8)
    for k in range(2):
        _row_copy(y_hbm.at[pl.ds(0, tc)], buf_ref.at[k], sem).wait()
    h_copy.wait()
    gates = gate_ref[0]
    y = (buf_ref[0].reshape(tc, D_MODEL) * gates[:, 0:1] + buf_ref[1].reshape(tc, D_MODEL) * gates[:, 1:2])
    o_ref[0] = _layer_norm(DEEPNORM_ALPHA * h_buf[...] + y, g_ref[...], b_ref[...])


def _combine(ys, dest, gates, h, g, bb, row0, seq):
    b = h.shape[0]
    tc = dest.shape[2] // 2
    nt = seq // tc
    par = lambda a: pl.BlockSpec(a.shape, lambda i: (0, 0))
    return pl.pallas_call(
        functools.partial(_combine_body, row0, nt),
        grid=(b * nt,),
        in_specs=[pl.BlockSpec((1, 1, 2 * tc), lambda i: (i, 0, 0), memory_space=pltpu.SMEM),
                  pl.BlockSpec((1, tc, LANE), lambda i: (i // nt, i % nt, 0)),
                  par(g), par(bb), pl.BlockSpec(memory_space=pl.ANY), pl.BlockSpec(memory_space=pl.ANY)],
        out_specs=pl.BlockSpec((1, tc, D_MODEL), lambda i: (i // nt, i % nt, 0)),
        out_shape=jax.ShapeDtypeStruct((b, seq, D_MODEL), F32),
        scratch_shapes=[pltpu.VMEM((2, tc) + ROW_TILE, F32), pltpu.VMEM((tc, D_MODEL), F32),
                        pltpu.SemaphoreType.DMA(()), pltpu.SemaphoreType.DMA(())],
        compiler_params=_cparams("arbitrary"),
        name="moe_combine_layernorm",
    )(dest, gates, g, bb, h, ys)


def _moe(h, gates, dest_lanes, counts, w_gu, w_down, g, bb, tm, row0):
    b, lp, _ = h.shape
    seq = lp - row0
    tc = max(t for t in range(SUBLANES, min(seq, 1024) + 1, SUBLANES) if seq % t == 0)
    tg = EXPERT_TILE
    n_tiles = -(-2 * b * lp // tg) + N_EXPERTS
    cnt = counts[0, :N_EXPERTS]
    tiles = (cnt + tg - 1) // tg
    ends = jnp.cumsum(tiles)
    group_row0 = (ends - tiles) * tg
    hot = dest_lanes[:, :, 0:2, None] == jnp.arange(N_EXPERTS, dtype=jnp.int32)
    dest = jnp.sum(jnp.where(hot, group_row0, 0), axis=-1) + dest_lanes[:, :, 2:4]
    dest_frames = dest[:, row0:].reshape(b * (seq // tc), 1, 2 * tc)
    dest = dest.reshape(b * (lp // tm), 1, 2 * tm)
    t = jnp.arange(n_tiles, dtype=jnp.int32)
    used = (t < ends[-1]).astype(jnp.int32)
    exp = jnp.minimum(jnp.searchsorted(ends, jnp.minimum(t, ends[-1] - 1), side="right"),
                      N_EXPERTS - 1).astype(jnp.int32)
    never = jnp.int32(n_tiles)
    tail = ends[-1] + jnp.arange(N_EXPERTS, dtype=jnp.int32)
    ztile = jnp.sort(jnp.concatenate([jnp.where(tiles > 0, ends - 1, never), jnp.where(tail < n_tiles, tail, never)]))
    nz = jnp.sum(ztile < never).astype(jnp.int32).reshape(1)
    xs = _dispatch(h, dest, ztile.astype(jnp.int32), nz, n_tiles, tg)
    ys = _experts(xs, exp, used, w_gu, w_down, tg)
    return _combine(ys, dest_frames, gates[:, row0:], h, g, bb, row0, seq)


def _rep(v):
    return jnp.repeat(v.astype(F32), HEAD_DIM)[None, :]


def _row(v):
    return v.astype(F32).reshape(1, -1)


def kernel(x, meta, ev_w_in, ev_conv_w, ev_conv_b, ev_dt_bias, ev_a_log, ev_d_skip, ev_ssm_norm, ev_shift_mu, ev_w0, ev_w_up, ev_a0, ev_a_up, ev_g_up, ev_k_k, ev_k_a, ev_r_k, ev_lnx_g, ev_lnx_b, ev_w_out, ev_ln1_g, ev_ln1_b, ev_ffn_w_gu, ev_ffn_w_down, ev_ln2_g, ev_ln2_b, od_w_in, od_conv_w, od_conv_b, od_gx_w, od_gx_b, od_ga_w, od_ga_b, od_lambda, od_w_out, od_ln1_g, od_ln1_b, od_router, od_exp_w_gu, od_exp_w_down, od_ln2_g, od_ln2_b):
    b, seq, d = x.shape
    assert d == D_MODEL
    l = seq + N_META
    pad = (-l) % CHUNK
    lp = l + pad
    assert lp % TIME_BLOCK == 0, lp
    tm = _row_tile(lp)

    h = jnp.concatenate([jnp.zeros((b, pad, d), x.dtype),
                         jnp.broadcast_to(meta.astype(x.dtype)[None], (b, N_META, d)), x], axis=1)
    hb = h.astype(BF16)

    i = 0
    w_in = ev_w_in[i]
    o1 = D_MODEL
    o2 = o1 + SSM_CONV_DIM
    o3 = o2 + N_HEADS
    z = _mm(hb, w_in[:, :o1].astype(BF16), F32, tm)
    xbc = _mm(hb, w_in[:, o1:o2].astype(BF16), F32, tm)
    dtx = _mm(hb, jnp.repeat(w_in[:, o2:o3], HEAD_DIM, axis=1).astype(BF16), F32, tm)
    cols = _mm(hb, w_in[:, o3:].astype(BF16), F32, tm)

    y_a = _ssd(z, xbc, dtx, ev_conv_w[i], _row(ev_conv_b[i]), _rep(ev_dt_bias[i]), _rep(ev_a_log[i]),
               _rep(ev_d_skip[i]), _row(ev_ssm_norm[i]), pad)

    zeros64 = jnp.zeros((64, D_MODEL), F32)
    wup = jnp.concatenate([ev_w_up[i], zeros64], axis=0).astype(BF16)
    aup = jnp.concatenate([zeros64, ev_a_up[i]], axis=0).astype(BF16)
    y_b = _rwkv(cols, _row(ev_shift_mu[i]), _row(ev_w0[i]), wup, _row(ev_a0[i]), aup, ev_g_up[i].astype(BF16),
                _row(ev_k_k[i]), _row(ev_k_a[i]), _row(ev_r_k[i]), _row(ev_lnx_g[i]), _row(ev_lnx_b[i]))

    w_out = ev_w_out[i].astype(BF16)
    h, hb = _proj_ln([y_a, y_b], [w_out[:D_MODEL], w_out[D_MODEL:]], h, _row(ev_ln1_g[i]), _row(ev_ln1_b[i]), tm)
    h, hb = _ffn(hb, h, ev_ffn_w_gu[i].astype(BF16), ev_ffn_w_down[i].astype(BF16),
                 _row(ev_ln2_g[i]), _row(ev_ln2_b[i]), tm, pad)

    proj = _mm(hb, od_w_in[i].astype(BF16), F32, tm)
    y_c = _rglru(proj, od_conv_w[i], _row(od_conv_b[i]), od_gx_w[i].astype(BF16), _row(od_gx_b[i]),
                 od_ga_w[i].astype(BF16), _row(od_ga_b[i]), _row(od_lambda[i]), pad)
    w_router = jnp.pad(od_router[i], ((0, 0), (0, LANE - N_EXPERTS)))
    h, gates, dest, counts = _proj_ln([y_c], [od_w_out[i].astype(BF16)], h, _row(od_ln1_g[i]), _row(od_ln1_b[i]), tm,
                                      w_router=w_router)
    return _moe(h, gates, dest, counts, od_exp_w_gu[i].astype(BF16), od_exp_w_down[i].astype(BF16),
                _row(od_ln2_g[i]), _row(od_ln2_b[i]), tm, pad + N_META)
```

```python
import functools
import math

import jax
import jax.numpy as jnp
from jax import lax
from jax.experimental import pallas as pl
from jax.experimental.pallas import tpu as pltpu

F32 = jnp.float32
BF16 = jnp.bfloat16

D_MODEL = 1024
N_META = 16
CHUNK = 64
HEAD_DIM = 64
N_HEADS = D_MODEL // HEAD_DIM
SSM_GROUPS = 2
SSM_STATE = 128
SSM_CONV_DIM = D_MODEL + 2 * SSM_GROUPS * SSM_STATE
CONV_K = 4
RWKV_LORA = 256
RWKV_COLS = 3 * D_MODEL + RWKV_LORA
RWKV_LN_EPS = 64e-5
LRU_BLOCKS = 8
LRU_BLOCK = D_MODEL // LRU_BLOCKS
LRU_C = 8.0
D_FF = 2816
N_EXPERTS = 8
DEPTH = 2
DEEPNORM_ALPHA = (2 * DEPTH) ** 0.25
LN_EPS = 1e-5

LANE = 128
SUBLANES = 8
ROW_TILE = (SUBLANES, LANE)
HALO = 8
GROUP_HEADS = 4
GROUP_W = GROUP_HEADS * HEAD_DIM
BD_SLOTS = 8
TIME_BLOCK = 192
FF_TILE = 1408
FFN_ROW_SPLIT = 2
RANK_SPLIT = 4
EXPERT_TILE = 512
EXPERT_FF_TILE = FF_TILE
VMEM_LIMIT = 56 * 1024 * 1024


def _cparams(*sem):
    return pltpu.CompilerParams(dimension_semantics=sem, vmem_limit_bytes=VMEM_LIMIT)


def _row_tile(lp):
    best = 16
    for t in range(16, min(lp, 1376) + 1, 16):
        if lp % t == 0:
            best = t
    return best


def _col_tile(n):
    for t in (512, 1664, 256, 128):
        if n % t == 0:
            return t
    raise ValueError(n)


def _sigmoid(x):
    return 1.0 / (1.0 + jnp.exp(-x))


def _silu(x):
    return x * _sigmoid(x)


def _softplus(x):
    return jnp.maximum(x, 0.0) + jnp.log(1.0 + jnp.exp(-jnp.abs(x)))


def _dot(a, b):
    return jnp.dot(a, b, preferred_element_type=F32)


def _dot_nt(a, b):
    return lax.dot_general(a, b, (((1,), (1,)), ((), ())), preferred_element_type=F32)


def _dot_tn(a, b):
    return lax.dot_general(a, b, (((0,), (0,)), ((), ())), preferred_element_type=F32)


def _prefix_sum(tri, x):
    hi = x.astype(BF16)
    r1 = x - hi.astype(F32)
    mid = r1.astype(BF16)
    lo = (r1 - mid.astype(F32)).astype(BF16)
    return _dot(tri, hi) + (_dot(tri, mid) + _dot(tri, lo))


def _layer_norm(x, g, b):
    mu = jnp.mean(x, axis=-1, keepdims=True)
    xc = x - mu
    var = jnp.mean(xc * xc, axis=-1, keepdims=True)
    return xc * lax.rsqrt(var + LN_EPS) * g + b


def _mm_body(x_ref, w_ref, o_ref):
    o_ref[0] = _dot(x_ref[0], w_ref[...]).astype(o_ref.dtype)


def _mm_cast_body(x_ref, w_ref, o_ref, xb_ref):
    @pl.when(pl.program_id(2) == 0)
    def _():
        xb_ref[0] = x_ref[0].astype(BF16)

    o_ref[0] = _dot(xb_ref[0], w_ref[...]).astype(o_ref.dtype)


def _mm(x, w, out_dtype, tm):
    b, lp, k = x.shape
    n = w.shape[1]
    tn = _col_tile(n)
    emit_copy = x.dtype != BF16
    x_spec = pl.BlockSpec((1, tm, k), lambda b_, i, j: (b_, i, 0))
    o_spec = pl.BlockSpec((1, tm, tn), lambda b_, i, j: (b_, i, j))
    o_shape = jax.ShapeDtypeStruct((b, lp, n), out_dtype)
    return pl.pallas_call(
        _mm_cast_body if emit_copy else _mm_body,
        grid=(b, lp // tm, n // tn),
        in_specs=[x_spec, pl.BlockSpec((k, tn), lambda b_, i, j: (0, j))],
        out_specs=[o_spec, x_spec] if emit_copy else o_spec,
        out_shape=[o_shape, jax.ShapeDtypeStruct((b, lp, k), BF16)] if emit_copy else o_shape,
        compiler_params=_cparams("parallel", "parallel", "arbitrary"),
        name="dense_matmul",
    )(x, w)


def _chunk_consts():
    row = lax.broadcasted_iota(jnp.int32, (CHUNK, GROUP_W), 0)
    col = lax.broadcasted_iota(jnp.int32, (CHUNK, GROUP_W), 1) % HEAD_DIM
    r2 = lax.broadcasted_iota(jnp.int32, (GROUP_W, GROUP_W), 0) // HEAD_DIM
    c2 = lax.broadcasted_iota(jnp.int32, (GROUP_W, GROUP_W), 1) // HEAD_DIM
    tri_r = lax.broadcasted_iota(jnp.int32, (CHUNK, CHUNK), 0)
    tri_c = lax.broadcasted_iota(jnp.int32, (CHUNK, CHUNK), 1)
    tri = jnp.where(tri_c <= tri_r, 1.0, 0.0).astype(BF16)
    return row, col, (r2 == c2), tri


def _block_diag(y, bd_mask):
    return jnp.where(bd_mask, jnp.concatenate([y] * GROUP_HEADS, axis=0), jnp.zeros((), y.dtype))


def _shift_halo(ext_ref, blk, first):
    tb = blk.shape[0]

    @pl.when(first)
    def _():
        ext_ref[0:HALO, :] = jnp.zeros((HALO, ext_ref.shape[1]), F32)

    @pl.when(jnp.logical_not(first))
    def _():
        ext_ref[0:HALO, :] = ext_ref[tb:tb + HALO, :]

    ext_ref[HALO:HALO + tb, :] = blk


def _ssd_body(pad, z_ref, xbc_ref, dtx_ref, cw_ref, cb_ref, dtb_ref, alog_ref, dsk_ref, nw_ref,
              y_ref, ext_ref, xc_ref, dt_ref, yb_ref, h_ref):
    tb = xbc_ref.shape[1]
    j = pl.program_id(1)

    @pl.when(j == 0)
    def _():
        h_ref[...] = jnp.zeros(h_ref.shape, F32)

    _shift_halo(ext_ref, xbc_ref[0], j == 0)
    acc = jnp.broadcast_to(cb_ref[...], (tb, SSM_CONV_DIM))
    for k in range(CONV_K):
        acc = acc + cw_ref[k:k + 1, :] * ext_ref[pl.ds(HALO - (CONV_K - 1) + k, tb), :]
    xc_ref[...] = _silu(acc)

    pos = j * tb + lax.broadcasted_iota(jnp.int32, (tb, 1), 0)
    dt_ref[...] = jnp.where(pos >= pad, _softplus(dtx_ref[0] + dtb_ref[...]), 0.0)

    row, col, bd_mask, tri = _chunk_consts()
    row_w = jnp.concatenate([row] * (D_MODEL // GROUP_W), axis=1)
    col_w = jnp.concatenate([col] * (D_MODEL // GROUP_W), axis=1)
    a_neg = -jnp.exp(alog_ref[...])
    gw = D_MODEL // SSM_GROUPS

    def chunk(c, carry):
        rows = pl.ds(pl.multiple_of(c * CHUNK, CHUNK), CHUNK)
        xs = xc_ref[rows, 0:D_MODEL]
        dt = dt_ref[rows, :]
        xdt = xs * dt
        acs = _prefix_sum(tri, dt * a_neg)
        acs_t = jnp.sum(jnp.where(row_w == col_w, acs, 0.0), axis=0, keepdims=True)
        decay = jnp.where(row_w >= col_w, jnp.exp(jnp.minimum(acs - acs_t, 0.0)), 0.0)
        e_acs = jnp.exp(acs)
        last = acs[CHUNK - 1:CHUNK, :]
        xend = xdt * jnp.exp(last - acs)
        e_last = jnp.exp(last)
        groups = range(SSM_GROUPS)
        gsl = [slice(g * gw, (g + 1) * gw) for g in groups]
        bg = [xc_ref[rows, D_MODEL + g * SSM_STATE:D_MODEL + (g + 1) * SSM_STATE].astype(BF16) for g in groups]
        cg = [xc_ref[rows, D_MODEL + (SSM_GROUPS + g) * SSM_STATE:
                     D_MODEL + (SSM_GROUPS + g + 1) * SSM_STATE].astype(BF16) for g in groups]
        scores = [_dot_nt(cg[g], jnp.concatenate([bg[g]] * GROUP_HEADS, axis=0)) for g in groups]
        hg = [h_ref[:, gsl[g]] for g in groups]
        y_off = [_dot(cg[g], hg[g].astype(BF16)) * e_acs[:, gsl[g]] for g in groups]
        st = [_dot_tn(bg[g], xend[:, gsl[g]].astype(BF16)) for g in groups]
        for g in groups:
            h_ref[:, gsl[g]] = hg[g] * e_last[:, gsl[g]] + st[g]
        for g in groups:
            for t in range(gw // GROUP_W):
                lo = g * gw + t * GROUP_W
                m = (scores[g] * decay[:, lo:lo + GROUP_W]).astype(BF16)
                xbd = _block_diag(xdt[:, lo:lo + GROUP_W].astype(BF16), bd_mask)
                yb_ref[rows, lo:lo + GROUP_W] = (_dot(m, xbd) + y_off[g][:, t * GROUP_W:(t + 1) * GROUP_W])
        return carry

    lax.fori_loop(0, tb // CHUNK, chunk, 0)

    y = (yb_ref[...] + dsk_ref[...] * xc_ref[:, 0:D_MODEL]) * _silu(z_ref[0])
    outs = []
    for g in range(SSM_GROUPS):
        yg = y[:, g * gw:(g + 1) * gw]
        ms = jnp.mean(yg * yg, axis=-1, keepdims=True)
        outs.append(yg * lax.rsqrt(ms + LN_EPS) * nw_ref[:, g * gw:(g + 1) * gw])
    y_ref[0] = jnp.concatenate(outs, axis=1).astype(y_ref.dtype)


def _ssd(z, xbc, dtx, cw, cb, dtb, alog, dsk, nw, pad):
    b, lp, _ = z.shape
    tb = TIME_BLOCK
    blk = lambda w: pl.BlockSpec((1, tb, w), lambda b_, j: (b_, j, 0))
    par = lambda a: pl.BlockSpec(a.shape, lambda b_, j: (0, 0))
    return pl.pallas_call(
        functools.partial(_ssd_body, pad),
        grid=(b, lp // tb),
        in_specs=[blk(D_MODEL), blk(SSM_CONV_DIM), blk(D_MODEL)] + [par(a) for a in (cw, cb, dtb, alog, dsk, nw)],
        out_specs=blk(D_MODEL),
        out_shape=jax.ShapeDtypeStruct((b, lp, D_MODEL), BF16),
        scratch_shapes=[pltpu.VMEM((tb + HALO, SSM_CONV_DIM), F32),
                        pltpu.VMEM((tb, SSM_CONV_DIM), F32),
                        pltpu.VMEM((tb, D_MODEL), F32),
                        pltpu.VMEM((tb, D_MODEL), F32),
                        pltpu.VMEM((SSM_STATE, D_MODEL), F32)],
        compiler_params=_cparams("parallel", "arbitrary"),
        name="ssd_branch",
    )(z, xbc, dtx, cw, cb, dtb, alog, dsk, nw)


def _head_sum(x, ones_bd):
    outs = []
    for i in range(D_MODEL // LANE):
        xi = x[:, i * LANE:(i + 1) * LANE]
        hi = xi.astype(BF16)
        lo = (xi - hi.astype(F32)).astype(BF16)
        outs.append(_dot(hi, ones_bd) + _dot(lo, ones_bd))
    return jnp.concatenate(outs, axis=1)


def _rwkv_body(cols_ref, mu_ref, w0_ref, wup_ref, a0_ref, aup_ref, gup_ref, kk_ref, ka_ref, rk_ref,
               lng_ref, lnb_ref, y_ref, ext_ref, r_s, k_s, v_s, kk_s, akk_s, lw_s, o_s, s_ref, bd_s):
    tb = cols_ref.shape[1]
    j = pl.program_id(1)

    @pl.when(j == 0)
    def _():
        s_ref[...] = jnp.zeros(s_ref.shape, F32)
        bd_s[...] = jnp.zeros(bd_s.shape, BF16)

    _shift_halo(ext_ref, cols_ref[0], j == 0)
    cur = ext_ref[HALO:HALO + tb, :]
    prev = ext_ref[pl.ds(HALO - 1, tb), :]
    mixed = cur + (prev - cur) * mu_ref[...]

    lane2 = lax.broadcasted_iota(jnp.int32, (LANE, LANE), 1) // HEAD_DIM
    row2 = lax.broadcasted_iota(jnp.int32, (LANE, LANE), 0) // HEAD_DIM
    ones_bd = (lane2 == row2).astype(BF16)

    r = mixed[:, 0:D_MODEL]
    k = mixed[:, D_MODEL:2 * D_MODEL]
    v = mixed[:, 2 * D_MODEL:3 * D_MODEL]
    lora_wa = mixed[:, 3 * D_MODEL:3 * D_MODEL + LANE]
    lora_g = mixed[:, 3 * D_MODEL + LANE:3 * D_MODEL + 2 * LANE]
    zw = w0_ref[...] + _dot(jnp.tanh(lora_wa).astype(BF16), wup_ref[...])
    a = _sigmoid(a0_ref[...] + _dot(lora_wa.astype(BF16), aup_ref[...]))
    gate = _dot(_sigmoid(lora_g).astype(BF16), gup_ref[...])
    kk = k * kk_ref[...]
    kk = kk * lax.rsqrt(jnp.maximum(_head_sum(kk * kk, ones_bd), 1e-24))
    kmod = k * (1.0 + (a - 1.0) * ka_ref[...])
    r_s[...] = r
    k_s[...] = kmod
    v_s[...] = v
    kk_s[...] = kk
    akk_s[...] = a * kk
    lw_s[...] = -math.exp(-0.5) * _sigmoid(zw)

    row, col, bd_mask, tri = _chunk_consts()
    eye = row == col
    strict = row > col
    incl = row >= col
    lvl_masks = []
    s = 2
    while s < CHUNK:
        lvl_masks.append((row // (2 * s) == col // (2 * s)) & ((row // s) % 2 == 1) & ((col // s) % 2 == 0))
        s *= 2
    pair = (row == col + 1) & (row % 2 == 1)

    bd_slot = [0]

    def block_diag(y):
        return _block_diag(y, bd_mask)

    def bdmm(x, y):
        return _dot(x.astype(BF16), block_diag(y.astype(BF16)))

    groups = range(D_MODEL // GROUP_W)
    sls = [slice(g * GROUP_W, (g + 1) * GROUP_W) for g in groups]

    def state_free(c):
        rows = slice(c * CHUNK, (c + 1) * CHUNK)
        lw = lw_s[rows, :]
        lc = _prefix_sum(tri, lw)
        e_p = jnp.exp(lc)
        e_m = jnp.exp(-lc)
        kkc = kk_s[rows, :]
        at = jnp.exp(lc - lw) * kkc
        bt = -(akk_s[rows, :] * e_m)
        kt = k_s[rows, :] * e_m
        rt = r_s[rows, :] * e_p
        p_last = e_p[CHUNK - 1:CHUNK, :]
        bp = bt * p_last
        kp = kt * p_last
        vc = v_s[rows, :]
        v_g = [vc[:, sl].astype(BF16) for sl in sls]
        lhs = [jnp.concatenate([at[:, sl].astype(BF16), rt[:, sl].astype(BF16)], axis=0) for sl in sls]
        g_b = [_dot_nt(lhs[g], block_diag(bt[:, sls[g]].astype(BF16))) for g in groups]
        g_k = [_dot_nt(lhs[g], block_diag(kt[:, sls[g]].astype(BF16))) for g in groups]
        a_ab = [jnp.where(strict, g_b[g][0:CHUNK], 0.0) for g in groups]
        a_rb = [jnp.where(incl, g_b[g][CHUNK:], 0.0) for g in groups]
        a_k = [jnp.where(jnp.concatenate([strict, incl], axis=0), g_k[g], 0.0) for g in groups]
        a_v = [bdmm(a_k[g], v_g[g]) for g in groups]
        yield
        inv = [jnp.where(eye, 1.0, 0.0) + jnp.where(pair, a_ab[g], 0.0) for g in groups]
        for m in lvl_masks:
            low = [bdmm(inv[g], jnp.where(m, a_ab[g], 0.0)) for g in groups]
            yield
            inv = [inv[g] + bdmm(low[g], inv[g]) for g in groups]
            yield
        bkp = [jnp.concatenate([bp[:, sl].astype(BF16), kp[:, sl].astype(BF16)], axis=0) for sl in sls]
        return dict(rows=rows, lhs=lhs, v_g=v_g, a_rb=a_rb, a_v=a_v, inv=inv, bkp=bkp, p_last=p_last)

    def state_dependent(ctx):
        s_q = [_dot_nt(ctx["lhs"][g], s_ref[g].astype(BF16)) for g in groups]
        yield
        u = [bdmm(ctx["inv"][g], s_q[g][0:CHUNK] + ctx["a_v"][g][0:CHUNK]) for g in groups]
        yield
        y = [s_q[g][CHUNK:] + ctx["a_v"][g][CHUNK:] + bdmm(ctx["a_rb"][g], u[g]) for g in groups]
        upd = [_dot_tn(jnp.concatenate([u[g].astype(BF16), ctx["v_g"][g]], axis=0), ctx["bkp"][g]) for g in groups]
        for g in groups:
            o_s[ctx["rows"], sls[g]] = y[g]
            s_ref[g] = s_ref[g] * ctx["p_last"][:, sls[g]] + jnp.where(bd_mask, upd[g], 0.0)

    def emit_interleaved(first, second):
        result, live = None, [g for g in (first, second) if g is not None]
        while live:
            for gen in list(live):
                try:
                    next(gen)
                except StopIteration as stop:
                    live.remove(gen)
                    if gen is first:
                        result = stop.value
        return result

    n_chunks = tb // CHUNK
    ctx = emit_interleaved(state_free(0), None)
    for c in range(n_chunks):
        nxt = state_free(c + 1) if c + 1 < n_chunks else None
        ctx_next = emit_interleaved(nxt, state_dependent(ctx))
        ctx = ctx_next

    o = o_s[...]
    mean = _head_sum(o, ones_bd) * (1.0 / HEAD_DIM)
    oc = o - mean
    var = _head_sum(oc * oc, ones_bd) * (1.0 / HEAD_DIM)
    o = oc * lax.rsqrt(var + RWKV_LN_EPS) * lng_ref[...] + lnb_ref[...]
    rr = r_s[...]
    bonus = _head_sum(rr * k_s[...] * rk_ref[...], ones_bd)
    o = o + bonus * v_s[...]
    y_ref[0] = (o * gate).astype(y_ref.dtype)


def _rwkv(cols, mu, w0, wup, a0, aup, gup, kk, ka, rk, lng, lnb):
    b, lp, _ = cols.shape
    tb = TIME_BLOCK
    par = lambda a: pl.BlockSpec(a.shape, lambda b_, j: (0, 0))
    params = (mu, w0, wup, a0, aup, gup, kk, ka, rk, lng, lnb)
    return pl.pallas_call(
        _rwkv_body,
        grid=(b, lp // tb),
        in_specs=[pl.BlockSpec((1, tb, RWKV_COLS), lambda b_, j: (b_, j, 0))] + [par(a) for a in params],
        out_specs=pl.BlockSpec((1, tb, D_MODEL), lambda b_, j: (b_, j, 0)),
        out_shape=jax.ShapeDtypeStruct((b, lp, D_MODEL), BF16),
        scratch_shapes=[pltpu.VMEM((tb + HALO, RWKV_COLS), F32)]
        + [pltpu.VMEM((tb, D_MODEL), F32) for _ in range(7)]
        + [pltpu.VMEM((D_MODEL // GROUP_W, GROUP_W, GROUP_W), F32),
           pltpu.VMEM((BD_SLOTS, GROUP_W, GROUP_W), BF16)],
        compiler_params=_cparams("parallel", "arbitrary"),
        name="rwkv7_branch",
    )(cols, *params)


def _gelu_tanh(x):
    return 0.5 * x * (1.0 + jnp.tanh(math.sqrt(2.0 / math.pi) * (x + 0.044715 * (x * x * x))))


def _rglru_body(pad, gb_ref, xr_ref, cw_ref, cb_ref, gxw_ref, gxb_ref, gaw_ref, gab_ref, lam_ref,
                y_ref, ext_ref, hc_ref):
    tb = xr_ref.shape[1]
    j = pl.program_id(1)

    @pl.when(j == 0)
    def _():
        hc_ref[...] = jnp.zeros(hc_ref.shape, F32)

    _shift_halo(ext_ref, xr_ref[0], j == 0)
    xf = jnp.broadcast_to(cb_ref[...], (tb, D_MODEL))
    for k in range(CONV_K):
        xf = xf + cw_ref[k:k + 1, :] * ext_ref[pl.ds(HALO - (CONV_K - 1) + k, tb), :]

    gx, ga = [], []
    for hblk in range(LRU_BLOCKS):
        xb = xf[:, hblk * LRU_BLOCK:(hblk + 1) * LRU_BLOCK].astype(BF16)
        gx.append(_dot(xb, gxw_ref[hblk]))
        ga.append(_dot(xb, gaw_ref[hblk]))
    gate_x = _sigmoid(jnp.concatenate(gx, axis=1) + gxb_ref[...])
    gate_a = _sigmoid(jnp.concatenate(ga, axis=1) + gab_ref[...])
    log_a = -LRU_C * gate_a * _softplus(-lam_ref[...])
    a = jnp.exp(log_a)
    u = jnp.sqrt(1.0 - jnp.exp(2.0 * log_a)) * (gate_x * xf)
    row = lax.broadcasted_iota(jnp.int32, (tb, 1), 0)
    u = jnp.where(j * tb + row >= pad, u, 0.0)

    a = a.reshape(tb // SUBLANES, SUBLANES, D_MODEL)
    u = u.reshape(tb // SUBLANES, SUBLANES, D_MODEL)
    sub = lax.broadcasted_iota(jnp.int32, (1, SUBLANES, 1), 1)
    d = 1
    while d < SUBLANES:
        keep = sub >= d
        a_sh = jnp.where(keep, pltpu.roll(a, d, 1), 1.0)
        u_sh = jnp.where(keep, pltpu.roll(u, d, 1), 0.0)
        u = u + a * u_sh
        a = a * a_sh
        d *= 2
    h = hc_ref[...]
    hs = []
    for grp in range(tb // SUBLANES):
        hs.append(a[grp] * h + u[grp])
        h = hs[-1][SUBLANES - 1:SUBLANES, :]
    hc_ref[...] = h
    y_ref[0] = (jnp.concatenate(hs, axis=0) * _gelu_tanh(gb_ref[0])).astype(y_ref.dtype)


def _rglru(proj, cw, cb, gxw, gxb, gaw, gab, lam, pad):
    b, lp, _ = proj.shape
    tb = TIME_BLOCK
    par2 = lambda a: pl.BlockSpec(a.shape, lambda b_, j: (0, 0))
    par3 = lambda a: pl.BlockSpec(a.shape, lambda b_, j: (0, 0, 0))
    return pl.pallas_call(
        functools.partial(_rglru_body, pad),
        grid=(b, lp // tb),
        in_specs=[pl.BlockSpec((1, tb, D_MODEL), lambda b_, j: (b_, j, 0)),
                  pl.BlockSpec((1, tb, D_MODEL), lambda b_, j: (b_, j, 1)),
                  par2(cw), par2(cb), par3(gxw), par2(gxb), par3(gaw), par2(gab), par2(lam)],
        out_specs=pl.BlockSpec((1, tb, D_MODEL), lambda b_, j: (b_, j, 0)),
        out_shape=jax.ShapeDtypeStruct((b, lp, D_MODEL), BF16),
        scratch_shapes=[pltpu.VMEM((tb + HALO, D_MODEL), F32), pltpu.VMEM((1, D_MODEL), F32)],
        compiler_params=_cparams("parallel", "arbitrary"),
        name="rglru_branch",
    )(proj, proj, cw, cb, gxw, gxb, gaw, gab, lam)


def _proj_ln_body(n_in, with_router, *refs):
    xs = refs[:n_in]
    ws = refs[n_in:2 * n_in]
    h_ref, g_ref, b_ref = refs[2 * n_in:2 * n_in + 3]
    rest = refs[2 * n_in + 3:]
    mix = _dot(xs[0][0], ws[0][...])
    for x_ref, w_ref in zip(xs[1:], ws[1:]):
        mix = mix + _dot(x_ref[0], w_ref[...])
    out = _layer_norm(DEEPNORM_ALPHA * h_ref[0] + mix, g_ref[...], b_ref[...])
    if not with_router:
        o_ref, ob_ref = rest
        ob_ref[0] = out.astype(BF16)
    else:
        wr_ref, o_ref, gate_ref, dest_ref, cnt_ref, base_ref = rest
        tm = out.shape[0]
        first = (pl.program_id(0) == 0) & (pl.program_id(1) == 0)

        @pl.when(first)
        def _():
            base_ref[...] = jnp.zeros(base_ref.shape, F32)

        o_hi = out.astype(BF16)
        o_lo = (out - o_hi.astype(F32)).astype(BF16)
        wr = wr_ref[...]
        w_hi = wr.astype(BF16)
        w_lo = (wr - w_hi.astype(F32)).astype(BF16)
        logits = _dot(o_hi, w_hi) + (_dot(o_lo, w_hi) + _dot(o_hi, w_lo))
        lane = lax.broadcasted_iota(jnp.int32, logits.shape, 1)
        logits = jnp.where(lane < N_EXPERTS, logits, -jnp.inf)
        m1 = jnp.max(logits, axis=-1, keepdims=True)
        i1 = jnp.min(jnp.where(logits == m1, lane, LANE), axis=-1, keepdims=True)
        rest_l = jnp.where(lane == i1, -jnp.inf, logits)
        m2 = jnp.max(rest_l, axis=-1, keepdims=True)
        i2 = jnp.min(jnp.where(rest_l == m2, lane, LANE), axis=-1, keepdims=True)
        e2 = jnp.exp(m2 - m1)
        gate_ref[0] = jnp.where(lane == 0, 1.0 / (1.0 + e2), 0.0) + jnp.where(lane == 1, e2 / (1.0 + e2), 0.0)
        hot1 = lane == i1
        hot2 = lane == i2
        both = jnp.where(hot1 | hot2, 1.0, 0.0)
        ts = tm // RANK_SPLIT
        tri_r = lax.broadcasted_iota(jnp.int32, (ts, ts), 0)
        tri_c = lax.broadcasted_iota(jnp.int32, (ts, ts), 1)
        tri = jnp.where(tri_c < tri_r, 1.0, 0.0).astype(BF16)
        base = base_ref[...]
        before = []
        for q in range(RANK_SPLIT):
            sub = both[q * ts:(q + 1) * ts]
            before.append(_dot(tri, sub.astype(BF16)) + base)
            base = base + jnp.sum(sub, axis=0, keepdims=True)
        before = jnp.concatenate(before, axis=0)
        rank1 = jnp.sum(jnp.where(hot1, before, 0.0), axis=-1, keepdims=True).astype(jnp.int32)
        rank2 = jnp.sum(jnp.where(hot2, before, 0.0), axis=-1, keepdims=True).astype(jnp.int32)
        dest_ref[0] = (jnp.where(lane == 0, i1, 0) + jnp.where(lane == 1, i2, 0)
                       + jnp.where(lane == 2, rank1, 0) + jnp.where(lane == 3, rank2, 0))
        base_ref[...] = base
        cnt_ref[...] = base.astype(jnp.int32)
    o_ref[0] = out


def _proj_ln(xs, ws, h, g, bb, tm, w_router=None):
    b, lp, _ = h.shape
    n_in = len(xs)
    row = lambda w: pl.BlockSpec((1, tm, w), lambda b_, i: (b_, i, 0))
    par = lambda a: pl.BlockSpec(a.shape, lambda b_, i: (0, 0))
    in_specs = [row(x.shape[2]) for x in xs] + [par(w) for w in ws] + [row(D_MODEL), par(g), par(bb)]
    args = list(xs) + list(ws) + [h, g, bb]
    if w_router is None:
        out_specs = [row(D_MODEL), row(D_MODEL)]
        out_shape = [jax.ShapeDtypeStruct((b, lp, D_MODEL), F32), jax.ShapeDtypeStruct((b, lp, D_MODEL), BF16)]
        scratch = []
        sem = ("parallel", "parallel")
    else:
        in_specs.append(par(w_router))
        args.append(w_router)
        out_specs = [row(D_MODEL), row(LANE), row(LANE), pl.BlockSpec((1, LANE), lambda b_, i: (0, 0))]
        out_shape = [jax.ShapeDtypeStruct((b, lp, D_MODEL), F32), jax.ShapeDtypeStruct((b, lp, LANE), F32),
                     jax.ShapeDtypeStruct((b, lp, LANE), jnp.int32), jax.ShapeDtypeStruct((1, LANE), jnp.int32)]
        scratch = [pltpu.VMEM((1, LANE), F32)]
        sem = ("arbitrary", "arbitrary")
    return pl.pallas_call(
        functools.partial(_proj_ln_body, n_in, w_router is not None),
        grid=(b, lp // tm),
        in_specs=in_specs, out_specs=out_specs, out_shape=out_shape, scratch_shapes=scratch,
        compiler_params=_cparams(*sem),
        name="out_proj_layernorm",
    )(*args)


def _ffn_body(pad, xb_ref, h_ref, wg_ref, wu_ref, wd_ref, g_ref, b_ref, o_ref, ob_ref, acc_ref):
    f = pl.program_id(2)
    tm = xb_ref.shape[1]

    @pl.when(f == 0)
    def _():
        acc_ref[...] = jnp.zeros(acc_ref.shape, F32)

    x = xb_ref[0]
    act = _silu(_dot(x, wg_ref[...])) * _dot(x, wu_ref[...])
    acc_ref[...] += _dot(act.astype(BF16), wd_ref[...])

    @pl.when(f == pl.num_programs(2) - 1)
    def _():
        out = _layer_norm(DEEPNORM_ALPHA * h_ref[0] + acc_ref[...], g_ref[...], b_ref[...])
        if pad:
            pos = pl.program_id(1) * tm + lax.broadcasted_iota(jnp.int32, (tm, 1), 0)
            out = jnp.where(pos >= pad, out, 0.0)
        o_ref[0] = out
        ob_ref[0] = out.astype(BF16)


def _ffn(xb, h, w_gu, w_down, g, bb, tm, zero_pad):
    b, lp, _ = h.shape
    nf = D_FF // FF_TILE
    tm = tm // FFN_ROW_SPLIT
    row = pl.BlockSpec((1, tm, D_MODEL), lambda b_, i, f: (b_, i, 0))
    par = lambda a: pl.BlockSpec(a.shape, lambda b_, i, f: (0, 0))
    return pl.pallas_call(
        functools.partial(_ffn_body, zero_pad),
        grid=(b, lp // tm, nf),
        in_specs=[row, row,
                  pl.BlockSpec((D_MODEL, FF_TILE), lambda b_, i, f: (0, f)),
                  pl.BlockSpec((D_MODEL, FF_TILE), lambda b_, i, f: (0, f + nf)),
                  pl.BlockSpec((FF_TILE, D_MODEL), lambda b_, i, f: (f, 0)),
                  par(g), par(bb)],
        out_specs=[row, row],
        out_shape=[jax.ShapeDtypeStruct((b, lp, D_MODEL), F32), jax.ShapeDtypeStruct((b, lp, D_MODEL), BF16)],
        scratch_shapes=[pltpu.VMEM((tm, D_MODEL), F32)],
        compiler_params=_cparams("parallel", "parallel", "arbitrary"),
        name="swiglu_layernorm",
    )(xb, h, w_gu, w_gu, w_down, g, bb)


def _row_copy(src, dst, sem):
    return pltpu.make_async_copy(src, dst, sem)


def _dispatch_body(tg, ztile_ref, nz_ref, dest_ref, x_ref, o_hbm, zbuf, xrow, sem, zsem):
    tm = x_ref.shape[1]

    @pl.when(pl.program_id(0) == 0)
    def _():
        zbuf[...] = jnp.zeros(zbuf.shape, F32)
        for phase in ("start", "wait"):
            for k in range(ztile_ref.shape[0]):
                @pl.when(k < nz_ref[0])
                def _():
                    copy = pltpu.make_async_copy(zbuf, o_hbm.at[pl.ds(ztile_ref[k] * tg, tg)], zsem)
                    copy.start() if phase == "start" else copy.wait()

    xrow[...] = x_ref[0].reshape(tm, SUBLANES, LANE)

    def issue(r, c):
        for k in range(2):
            _row_copy(xrow.at[pl.ds(r, 1)], o_hbm.at[pl.ds(dest_ref[0, 0, 2 * r + k], 1)], sem).start(priority=k)
        return c

    lax.fori_loop(0, tm, issue, 0, unroll=8)
    for k in range(2):
        _row_copy(xrow, o_hbm.at[pl.ds(0, tm)], sem).wait()


def _dispatch(h, dest, ztile, nz, n_tiles, tg):
    b, lp, _ = h.shape
    tm = dest.shape[2] // 2
    nt = lp // tm
    grid_spec = pltpu.PrefetchScalarGridSpec(
        num_scalar_prefetch=2,
        grid=(b * nt,),
        in_specs=[pl.BlockSpec((1, 1, 2 * tm), lambda i, zt, nz_: (i, 0, 0), memory_space=pltpu.SMEM),
                  pl.BlockSpec((1, tm, D_MODEL), lambda i, zt, nz_: (i // nt, i % nt, 0))],
        out_specs=pl.BlockSpec(memory_space=pl.ANY),
        scratch_shapes=[pltpu.VMEM((tg,) + ROW_TILE, F32), pltpu.VMEM((tm,) + ROW_TILE, F32),
                        pltpu.SemaphoreType.DMA(()), pltpu.SemaphoreType.DMA(())],
    )
    return pl.pallas_call(
        functools.partial(_dispatch_body, tg),
        grid_spec=grid_spec,
        out_shape=jax.ShapeDtypeStruct((n_tiles * tg,) + ROW_TILE, F32),
        compiler_params=_cparams("arbitrary"),
        name="moe_dispatch",
    )(ztile, nz, dest, h)


def _experts_body(exp_ref, used_ref, x_ref, wg_ref, wu_ref, wd_ref, o_ref, acc_ref, xb_ref):
    t = pl.program_id(0)
    f = pl.program_id(1)
    last = f == pl.num_programs(1) - 1

    @pl.when(used_ref[t] > 0)
    def _():
        @pl.when(f == 0)
        def _():
            acc_ref[...] = jnp.zeros(acc_ref.shape, F32)
            xb_ref[...] = x_ref[...].reshape(xb_ref.shape).astype(BF16)

        x = xb_ref[...]
        act = _silu(_dot(x, wg_ref[0])) * _dot(x, wu_ref[0])
        acc_ref[...] += _dot(act.astype(BF16), wd_ref[0])

        @pl.when(last)
        def _():
            o_ref[...] = acc_ref[...].reshape(o_ref.shape)

    @pl.when((used_ref[t] == 0) & last)
    def _():
        o_ref[...] = jnp.zeros(o_ref.shape, F32)


def _experts(xs, exp, used, w_gu, w_down, tg):
    tf = EXPERT_FF_TILE
    nf = D_FF // tf
    n_tiles = exp.shape[0]
    ff = lambda t, f, used_: jnp.where(used_[t] > 0, f, nf - 1)
    grid_spec = pltpu.PrefetchScalarGridSpec(
        num_scalar_prefetch=2,
        grid=(n_tiles, nf),
        in_specs=[pl.BlockSpec((tg,) + ROW_TILE, lambda t, f, exp_, used_: (t, 0, 0)),
                  pl.BlockSpec((1, D_MODEL, tf), lambda t, f, exp_, used_: (exp_[t], 0, ff(t, f, used_))),
                  pl.BlockSpec((1, D_MODEL, tf), lambda t, f, exp_, used_: (exp_[t], 0, ff(t, f, used_) + nf)),
                  pl.BlockSpec((1, tf, D_MODEL), lambda t, f, exp_, used_: (exp_[t], ff(t, f, used_), 0))],
        out_specs=pl.BlockSpec((tg,) + ROW_TILE, lambda t, f, exp_, used_: (t, 0, 0)),
        scratch_shapes=[pltpu.VMEM((tg, D_MODEL), F32), pltpu.VMEM((tg, D_MODEL), BF16)],
    )
    return pl.pallas_call(
        _experts_body,
        grid_spec=grid_spec,
        out_shape=jax.ShapeDtypeStruct(xs.shape, F32),
        compiler_params=_cparams("arbitrary", "arbitrary"),
        name="moe_experts",
    )(exp, used, xs, w_gu, w_gu, w_down)


def _combine_body(row0, nt, dest_ref, dest_next_ref, gate_ref, g_ref, b_ref, h_hbm, y_hbm, o_ref,
                  buf_ref, h_buf, sem, h_sem):
    tc = o_ref.shape[1]
    i = pl.program_id(0)
    n = pl.num_programs(0)
    cur = i % 2

    def h_copy(tile, slot):
        return pltpu.make_async_copy(h_hbm.at[tile // nt, pl.ds(row0 + (tile % nt) * tc, tc)],
                                     h_buf.at[slot], h_sem.at[slot])

    def start_tile(tile, slot, rows_ref):
        h_copy(tile, slot).start()

        def issue(r, c):
            for k in range(2):
                _row_copy(y_hbm.at[pl.ds(rows_ref[0, 0, 2 * r + k], 1)], buf_ref.at[slot, k, pl.ds(r, 1)],
                          sem.at[slot]).start(priority=k)
            return c

        lax.fori_loop(0, tc, issue, 0, unroll=8)

    @pl.when(i == 0)
    def _():
        start_tile(i, cur, dest_ref)

    @pl.when(i + 1 < n)
    def _():
        start_tile(i + 1, 1 - cur, dest_next_ref)

    for k in range(2):
        _row_copy(y_hbm.at[pl.ds(0, tc)], buf_ref.at[cur, k], sem.at[cur]).wait()
    h_copy(i, cur).wait()
    gates = gate_ref[0]
    y = (buf_ref[cur, 0].reshape(tc, D_MODEL) * gates[:, 0:1] + buf_ref[cur, 1].reshape(tc, D_MODEL) * gates[:, 1:2])
    o_ref[0] = _layer_norm(DEEPNORM_ALPHA * h_buf[cur] + y, g_ref[...], b_ref[...])


def _combine(ys, dest, gates, h, g, bb, row0, seq):
    b = h.shape[0]
    tc = dest.shape[2] // 2
    nt = seq // tc
    par = lambda a: pl.BlockSpec(a.shape, lambda i: (0, 0))
    n_steps = b * nt
    return pl.pallas_call(
        functools.partial(_combine_body, row0, nt),
        grid=(n_steps,),
        in_specs=[pl.BlockSpec((1, 1, 2 * tc), lambda i: (i, 0, 0), memory_space=pltpu.SMEM),
                  pl.BlockSpec((1, 1, 2 * tc), lambda i: (jnp.minimum(i + 1, n_steps - 1), 0, 0),
                               memory_space=pltpu.SMEM),
                  pl.BlockSpec((1, tc, LANE), lambda i: (i // nt, i % nt, 0)),
                  par(g), par(bb), pl.BlockSpec(memory_space=pl.ANY), pl.BlockSpec(memory_space=pl.ANY)],
        out_specs=pl.BlockSpec((1, tc, D_MODEL), lambda i: (i // nt, i % nt, 0)),
        out_shape=jax.ShapeDtypeStruct((b, seq, D_MODEL), F32),
        scratch_shapes=[pltpu.VMEM((2, 2, tc) + ROW_TILE, F32), pltpu.VMEM((2, tc, D_MODEL), F32),
                        pltpu.SemaphoreType.DMA((2,)), pltpu.SemaphoreType.DMA((2,))],
        compiler_params=_cparams("arbitrary"),
        name="moe_combine_layernorm",
    )(dest, dest, gates, g, bb, h, ys)


def _moe(h, gates, dest_lanes, counts, w_gu, w_down, g, bb, tm, row0):
    b, lp, _ = h.shape
    seq = lp - row0
    tc = max(t for t in range(SUBLANES, min(seq, 1024) + 1, SUBLANES) if seq % t == 0)
    tg = EXPERT_TILE
    n_tiles = -(-2 * b * lp // tg) + N_EXPERTS
    cnt = counts[0, :N_EXPERTS]
    tiles = (cnt + tg - 1) // tg
    ends = jnp.cumsum(tiles)
    group_row0 = (ends - tiles) * tg
    hot = dest_lanes[:, :, 0:2, None] == jnp.arange(N_EXPERTS, dtype=jnp.int32)
    dest = jnp.sum(jnp.where(hot, group_row0, 0), axis=-1) + dest_lanes[:, :, 2:4]
    dest_frames = dest[:, row0:].reshape(b * (seq // tc), 1, 2 * tc)
    dest = dest.reshape(b * (lp // tm), 1, 2 * tm)
    t = jnp.arange(n_tiles, dtype=jnp.int32)
    used = (t < ends[-1]).astype(jnp.int32)
    exp = jnp.minimum(jnp.searchsorted(ends, jnp.minimum(t, ends[-1] - 1), side="right"),
                      N_EXPERTS - 1).astype(jnp.int32)
    never = jnp.int32(n_tiles)
    tail = ends[-1] + jnp.arange(N_EXPERTS, dtype=jnp.int32)
    ztile = jnp.sort(jnp.concatenate([jnp.where(tiles > 0, ends - 1, never), jnp.where(tail < n_tiles, tail, never)]))
    nz = jnp.sum(ztile < never).astype(jnp.int32).reshape(1)
    xs = _dispatch(h, dest, ztile.astype(jnp.int32), nz, n_tiles, tg)
    ys = _experts(xs, exp, used, w_gu, w_down, tg)
    return _combine(ys, dest_frames, gates[:, row0:], h, g, bb, row0, seq)


def _rep(v):
    return jnp.repeat(v.astype(F32), HEAD_DIM)[None, :]


def _row(v):
    return v.astype(F32).reshape(1, -1)


def kernel(x, meta, ev_w_in, ev_conv_w, ev_conv_b, ev_dt_bias, ev_a_log, ev_d_skip, ev_ssm_norm, ev_shift_mu, ev_w0, ev_w_up, ev_a0, ev_a_up, ev_g_up, ev_k_k, ev_k_a, ev_r_k, ev_lnx_g, ev_lnx_b, ev_w_out, ev_ln1_g, ev_ln1_b, ev_ffn_w_gu, ev_ffn_w_down, ev_ln2_g, ev_ln2_b, od_w_in, od_conv_w, od_conv_b, od_gx_w, od_gx_b, od_ga_w, od_ga_b, od_lambda, od_w_out, od_ln1_g, od_ln1_b, od_router, od_exp_w_gu, od_exp_w_down, od_ln2_g, od_ln2_b):
    b, seq, d = x.shape
    assert d == D_MODEL
    l = seq + N_META
    pad = (-l) % CHUNK
    lp = l + pad
    assert lp % TIME_BLOCK == 0, lp
    tm = _row_tile(lp)

    h = jnp.concatenate([jnp.zeros((b, pad, d), x.dtype),
                         jnp.broadcast_to(meta.astype(x.dtype)[None], (b, N_META, d)), x], axis=1)

    i = 0
    w_in = ev_w_in[i]
    o1 = D_MODEL
    o2 = o1 + SSM_CONV_DIM
    o3 = o2 + N_HEADS
    z, hb = _mm(h, w_in[:, :o1].astype(BF16), F32, tm)
    xbc = _mm(hb, w_in[:, o1:o2].astype(BF16), F32, tm)
    dtx = _mm(hb, jnp.repeat(w_in[:, o2:o3], HEAD_DIM, axis=1).astype(BF16), F32, tm)
    cols = _mm(hb, w_in[:, o3:].astype(BF16), F32, tm)

    y_a = _ssd(z, xbc, dtx, ev_conv_w[i], _row(ev_conv_b[i]), _rep(ev_dt_bias[i]), _rep(ev_a_log[i]),
               _rep(ev_d_skip[i]), _row(ev_ssm_norm[i]), pad)

    zeros64 = jnp.zeros((64, D_MODEL), F32)
    wup = jnp.concatenate([ev_w_up[i], zeros64], axis=0).astype(BF16)
    aup = jnp.concatenate([zeros64, ev_a_up[i]], axis=0).astype(BF16)
    y_b = _rwkv(cols, _row(ev_shift_mu[i]), _row(ev_w0[i]), wup, _row(ev_a0[i]), aup, ev_g_up[i].astype(BF16),
                _row(ev_k_k[i]), _row(ev_k_a[i]), _row(ev_r_k[i]), _row(ev_lnx_g[i]), _row(ev_lnx_b[i]))

    w_out = ev_w_out[i].astype(BF16)
    h, hb = _proj_ln([y_a, y_b], [w_out[:D_MODEL], w_out[D_MODEL:]], h, _row(ev_ln1_g[i]), _row(ev_ln1_b[i]), tm)
    h, hb = _ffn(hb, h, ev_ffn_w_gu[i].astype(BF16), ev_ffn_w_down[i].astype(BF16),
                 _row(ev_ln2_g[i]), _row(ev_ln2_b[i]), tm, pad)

    proj = _mm(hb, od_w_in[i].astype(BF16), F32, tm)
    y_c = _rglru(proj, od_conv_w[i], _row(od_conv_b[i]), od_gx_w[i].astype(BF16), _row(od_gx_b[i]),
                 od_ga_w[i].astype(BF16), _row(od_ga_b[i]), _row(od_lambda[i]), pad)
    w_router = jnp.pad(od_router[i], ((0, 0), (0, LANE - N_EXPERTS)))
    h, gates, dest, counts = _proj_ln([y_c], [od_w_out[i].astype(BF16)], h, _row(od_ln1_g[i]), _row(od_ln1_b[i]), tm,
                                      w_router=w_router)
    return _moe(h, gates, dest, counts, od_exp_w_gu[i].astype(BF16), od_exp_w_down[i].astype(BF16),
                _row(od_ln2_g[i]), _row(od_ln2_b[i]), tm, pad + N_META)
```

```python
import functools
import math

import jax
import jax.numpy as jnp
from jax import lax
from jax.experimental import pallas as pl
from jax.experimental.pallas import tpu as pltpu

F32 = jnp.float32
BF16 = jnp.bfloat16

D_MODEL = 1024
N_META = 16
CHUNK = 64
HEAD_DIM = 64
N_HEADS = D_MODEL // HEAD_DIM
SSM_GROUPS = 2
SSM_STATE = 128
SSM_CONV_DIM = D_MODEL + 2 * SSM_GROUPS * SSM_STATE
CONV_K = 4
RWKV_LORA = 256
RWKV_COLS = 3 * D_MODEL + RWKV_LORA
RWKV_LN_EPS = 64e-5
LRU_BLOCKS = 8
LRU_BLOCK = D_MODEL // LRU_BLOCKS
LRU_C = 8.0
D_FF = 2816
N_EXPERTS = 8
DEPTH = 2
DEEPNORM_ALPHA = (2 * DEPTH) ** 0.25
LN_EPS = 1e-5

LANE = 128
SUBLANES = 8
ROW_TILE = (SUBLANES, LANE)
HALO = 8
GROUP_HEADS = 4
GROUP_W = GROUP_HEADS * HEAD_DIM
BD_SLOTS = 8
TIME_BLOCK = 192
FF_TILE = 1408
FFN_ROW_SPLIT = 2
RANK_SPLIT = 4
EXPERT_TILE = 768
EXPERT_FF_TILE = FF_TILE
VMEM_LIMIT = 56 * 1024 * 1024


def _cparams(*sem):
    return pltpu.CompilerParams(dimension_semantics=sem, vmem_limit_bytes=VMEM_LIMIT)


def _row_tile(lp):
    best = 16
    for t in range(16, min(lp, 1376) + 1, 16):
        if lp % t == 0:
            best = t
    return best


def _col_tile(n):
    for t in (512, 1664, 256, 128):
        if n % t == 0:
            return t
    raise ValueError(n)


def _sigmoid(x):
    return 1.0 / (1.0 + jnp.exp(-x))


def _silu(x):
    return x * _sigmoid(x)


def _softplus(x):
    return jnp.maximum(x, 0.0) + jnp.log(1.0 + jnp.exp(-jnp.abs(x)))


def _dot(a, b):
    return jnp.dot(a, b, preferred_element_type=F32)


def _dot_nt(a, b):
    return lax.dot_general(a, b, (((1,), (1,)), ((), ())), preferred_element_type=F32)


def _dot_tn(a, b):
    return lax.dot_general(a, b, (((0,), (0,)), ((), ())), preferred_element_type=F32)


def _prefix_sum(tri, x):
    hi = x.astype(BF16)
    r1 = x - hi.astype(F32)
    mid = r1.astype(BF16)
    lo = (r1 - mid.astype(F32)).astype(BF16)
    return _dot(tri, hi) + (_dot(tri, mid) + _dot(tri, lo))


def _layer_norm(x, g, b):
    mu = jnp.mean(x, axis=-1, keepdims=True)
    xc = x - mu
    var = jnp.mean(xc * xc, axis=-1, keepdims=True)
    return xc * lax.rsqrt(var + LN_EPS) * g + b


def _mm_body(x_ref, w_ref, o_ref):
    o_ref[0] = _dot(x_ref[0], w_ref[...]).astype(o_ref.dtype)


def _mm_cast_body(x_ref, w_ref, o_ref, xb_ref):
    @pl.when(pl.program_id(2) == 0)
    def _():
        xb_ref[0] = x_ref[0].astype(BF16)

    o_ref[0] = _dot(xb_ref[0], w_ref[...]).astype(o_ref.dtype)


def _mm(x, w, out_dtype, tm):
    b, lp, k = x.shape
    n = w.shape[1]
    tn = _col_tile(n)
    emit_copy = x.dtype != BF16
    x_spec = pl.BlockSpec((1, tm, k), lambda b_, i, j: (b_, i, 0))
    o_spec = pl.BlockSpec((1, tm, tn), lambda b_, i, j: (b_, i, j))
    o_shape = jax.ShapeDtypeStruct((b, lp, n), out_dtype)
    return pl.pallas_call(
        _mm_cast_body if emit_copy else _mm_body,
        grid=(b, lp // tm, n // tn),
        in_specs=[x_spec, pl.BlockSpec((k, tn), lambda b_, i, j: (0, j))],
        out_specs=[o_spec, x_spec] if emit_copy else o_spec,
        out_shape=[o_shape, jax.ShapeDtypeStruct((b, lp, k), BF16)] if emit_copy else o_shape,
        compiler_params=_cparams("parallel", "parallel", "arbitrary"),
        name="dense_matmul",
    )(x, w)


def _chunk_consts():
    row = lax.broadcasted_iota(jnp.int32, (CHUNK, GROUP_W), 0)
    col = lax.broadcasted_iota(jnp.int32, (CHUNK, GROUP_W), 1) % HEAD_DIM
    r2 = lax.broadcasted_iota(jnp.int32, (GROUP_W, GROUP_W), 0) // HEAD_DIM
    c2 = lax.broadcasted_iota(jnp.int32, (GROUP_W, GROUP_W), 1) // HEAD_DIM
    tri_r = lax.broadcasted_iota(jnp.int32, (CHUNK, CHUNK), 0)
    tri_c = lax.broadcasted_iota(jnp.int32, (CHUNK, CHUNK), 1)
    tri = jnp.where(tri_c <= tri_r, 1.0, 0.0).astype(BF16)
    return row, col, (r2 == c2), tri


def _block_diag(y, bd_mask):
    return jnp.where(bd_mask, jnp.concatenate([y] * GROUP_HEADS, axis=0), jnp.zeros((), y.dtype))


def _shift_halo(ext_ref, blk, first):
    tb = blk.shape[0]

    @pl.when(first)
    def _():
        ext_ref[0:HALO, :] = jnp.zeros((HALO, ext_ref.shape[1]), F32)

    @pl.when(jnp.logical_not(first))
    def _():
        ext_ref[0:HALO, :] = ext_ref[tb:tb + HALO, :]

    ext_ref[HALO:HALO + tb, :] = blk


def _ssd_body(pad, z_ref, xbc_ref, dtx_ref, cw_ref, cb_ref, dtb_ref, alog_ref, dsk_ref, nw_ref,
              y_ref, ext_ref, xc_ref, dt_ref, yb_ref, h_ref):
    tb = xbc_ref.shape[1]
    j = pl.program_id(1)

    @pl.when(j == 0)
    def _():
        h_ref[...] = jnp.zeros(h_ref.shape, F32)

    _shift_halo(ext_ref, xbc_ref[0], j == 0)
    acc = jnp.broadcast_to(cb_ref[...], (tb, SSM_CONV_DIM))
    for k in range(CONV_K):
        acc = acc + cw_ref[k:k + 1, :] * ext_ref[pl.ds(HALO - (CONV_K - 1) + k, tb), :]
    xc_ref[...] = _silu(acc)

    pos = j * tb + lax.broadcasted_iota(jnp.int32, (tb, 1), 0)
    dt_ref[...] = jnp.where(pos >= pad, _softplus(dtx_ref[0] + dtb_ref[...]), 0.0)

    row, col, bd_mask, tri = _chunk_consts()
    row_w = jnp.concatenate([row] * (D_MODEL // GROUP_W), axis=1)
    col_w = jnp.concatenate([col] * (D_MODEL // GROUP_W), axis=1)
    a_neg = -jnp.exp(alog_ref[...])
    gw = D_MODEL // SSM_GROUPS

    def chunk(c, carry):
        rows = pl.ds(pl.multiple_of(c * CHUNK, CHUNK), CHUNK)
        xs = xc_ref[rows, 0:D_MODEL]
        dt = dt_ref[rows, :]
        xdt = xs * dt
        acs = _prefix_sum(tri, dt * a_neg)
        acs_t = jnp.sum(jnp.where(row_w == col_w, acs, 0.0), axis=0, keepdims=True)
        decay = jnp.where(row_w >= col_w, jnp.exp(jnp.minimum(acs - acs_t, 0.0)), 0.0)
        e_acs = jnp.exp(acs)
        last = acs[CHUNK - 1:CHUNK, :]
        xend = xdt * jnp.exp(last - acs)
        e_last = jnp.exp(last)
        groups = range(SSM_GROUPS)
        gsl = [slice(g * gw, (g + 1) * gw) for g in groups]
        bg = [xc_ref[rows, D_MODEL + g * SSM_STATE:D_MODEL + (g + 1) * SSM_STATE].astype(BF16) for g in groups]
        cg = [xc_ref[rows, D_MODEL + (SSM_GROUPS + g) * SSM_STATE:
                     D_MODEL + (SSM_GROUPS + g + 1) * SSM_STATE].astype(BF16) for g in groups]
        scores = [_dot_nt(cg[g], jnp.concatenate([bg[g]] * GROUP_HEADS, axis=0)) for g in groups]
        hg = [h_ref[:, gsl[g]] for g in groups]
        y_off = [_dot(cg[g], hg[g].astype(BF16)) * e_acs[:, gsl[g]] for g in groups]
        st = [_dot_tn(bg[g], xend[:, gsl[g]].astype(BF16)) for g in groups]
        for g in groups:
            h_ref[:, gsl[g]] = hg[g] * e_last[:, gsl[g]] + st[g]
        for g in groups:
            for t in range(gw // GROUP_W):
                lo = g * gw + t * GROUP_W
                m = (scores[g] * decay[:, lo:lo + GROUP_W]).astype(BF16)
                xbd = _block_diag(xdt[:, lo:lo + GROUP_W].astype(BF16), bd_mask)
                yb_ref[rows, lo:lo + GROUP_W] = (_dot(m, xbd) + y_off[g][:, t * GROUP_W:(t + 1) * GROUP_W])
        return carry

    lax.fori_loop(0, tb // CHUNK, chunk, 0)

    y = (yb_ref[...] + dsk_ref[...] * xc_ref[:, 0:D_MODEL]) * _silu(z_ref[0])
    outs = []
    for g in range(SSM_GROUPS):
        yg = y[:, g * gw:(g + 1) * gw]
        ms = jnp.mean(yg * yg, axis=-1, keepdims=True)
        outs.append(yg * lax.rsqrt(ms + LN_EPS) * nw_ref[:, g * gw:(g + 1) * gw])
    y_ref[0] = jnp.concatenate(outs, axis=1).astype(y_ref.dtype)


def _ssd(z, xbc, dtx, cw, cb, dtb, alog, dsk, nw, pad):
    b, lp, _ = z.shape
    tb = TIME_BLOCK
    blk = lambda w: pl.BlockSpec((1, tb, w), lambda b_, j: (b_, j, 0))
    par = lambda a: pl.BlockSpec(a.shape, lambda b_, j: (0, 0))
    return pl.pallas_call(
        functools.partial(_ssd_body, pad),
        grid=(b, lp // tb),
        in_specs=[blk(D_MODEL), blk(SSM_CONV_DIM), blk(D_MODEL)] + [par(a) for a in (cw, cb, dtb, alog, dsk, nw)],
        out_specs=blk(D_MODEL),
        out_shape=jax.ShapeDtypeStruct((b, lp, D_MODEL), BF16),
        scratch_shapes=[pltpu.VMEM((tb + HALO, SSM_CONV_DIM), F32),
                        pltpu.VMEM((tb, SSM_CONV_DIM), F32),
                        pltpu.VMEM((tb, D_MODEL), F32),
                        pltpu.VMEM((tb, D_MODEL), F32),
                        pltpu.VMEM((SSM_STATE, D_MODEL), F32)],
        compiler_params=_cparams("parallel", "arbitrary"),
        name="ssd_branch",
    )(z, xbc, dtx, cw, cb, dtb, alog, dsk, nw)


def _head_sum(x, ones_bd):
    outs = []
    for i in range(D_MODEL // LANE):
        xi = x[:, i * LANE:(i + 1) * LANE]
        hi = xi.astype(BF16)
        lo = (xi - hi.astype(F32)).astype(BF16)
        outs.append(_dot(hi, ones_bd) + _dot(lo, ones_bd))
    return jnp.concatenate(outs, axis=1)


def _rwkv_body(cols_ref, mu_ref, w0_ref, wup_ref, a0_ref, aup_ref, gup_ref, kk_ref, ka_ref, rk_ref,
               lng_ref, lnb_ref, y_ref, ext_ref, r_s, k_s, v_s, kk_s, akk_s, lw_s, o_s, s_ref, bd_s):
    tb = cols_ref.shape[1]
    j = pl.program_id(1)

    @pl.when(j == 0)
    def _():
        s_ref[...] = jnp.zeros(s_ref.shape, F32)
        bd_s[...] = jnp.zeros(bd_s.shape, BF16)

    _shift_halo(ext_ref, cols_ref[0], j == 0)
    cur = ext_ref[HALO:HALO + tb, :]
    prev = ext_ref[pl.ds(HALO - 1, tb), :]
    mixed = cur + (prev - cur) * mu_ref[...]

    lane2 = lax.broadcasted_iota(jnp.int32, (LANE, LANE), 1) // HEAD_DIM
    row2 = lax.broadcasted_iota(jnp.int32, (LANE, LANE), 0) // HEAD_DIM
    ones_bd = (lane2 == row2).astype(BF16)

    r = mixed[:, 0:D_MODEL]
    k = mixed[:, D_MODEL:2 * D_MODEL]
    v = mixed[:, 2 * D_MODEL:3 * D_MODEL]
    lora_wa = mixed[:, 3 * D_MODEL:3 * D_MODEL + LANE]
    lora_g = mixed[:, 3 * D_MODEL + LANE:3 * D_MODEL + 2 * LANE]
    zw = w0_ref[...] + _dot(jnp.tanh(lora_wa).astype(BF16), wup_ref[...])
    a = _sigmoid(a0_ref[...] + _dot(lora_wa.astype(BF16), aup_ref[...]))
    gate = _dot(_sigmoid(lora_g).astype(BF16), gup_ref[...])
    kk = k * kk_ref[...]
    kk = kk * lax.rsqrt(jnp.maximum(_head_sum(kk * kk, ones_bd), 1e-24))
    kmod = k * (1.0 + (a - 1.0) * ka_ref[...])
    r_s[...] = r
    k_s[...] = kmod
    v_s[...] = v
    kk_s[...] = kk
    akk_s[...] = a * kk
    lw_s[...] = -math.exp(-0.5) * _sigmoid(zw)

    row, col, bd_mask, tri = _chunk_consts()
    eye = row == col
    strict = row > col
    incl = row >= col
    lvl_masks = []
    s = 2
    while s < CHUNK:
        lvl_masks.append((row // (2 * s) == col // (2 * s)) & ((row // s) % 2 == 1) & ((col // s) % 2 == 0))
        s *= 2
    pair = (row == col + 1) & (row % 2 == 1)

    bd_slot = [0]

    def block_diag(y):
        return _block_diag(y, bd_mask)

    def bdmm(x, y):
        return _dot(x.astype(BF16), block_diag(y.astype(BF16)))

    groups = range(D_MODEL // GROUP_W)
    sls = [slice(g * GROUP_W, (g + 1) * GROUP_W) for g in groups]

    def state_free(c):
        rows = slice(c * CHUNK, (c + 1) * CHUNK)
        lw = lw_s[rows, :]
        lc = _prefix_sum(tri, lw)
        e_p = jnp.exp(lc)
        e_m = jnp.exp(-lc)
        kkc = kk_s[rows, :]
        at = jnp.exp(lc - lw) * kkc
        bt = -(akk_s[rows, :] * e_m)
        kt = k_s[rows, :] * e_m
        rt = r_s[rows, :] * e_p
        p_last = e_p[CHUNK - 1:CHUNK, :]
        bp = bt * p_last
        kp = kt * p_last
        vc = v_s[rows, :]
        v_g = [vc[:, sl].astype(BF16) for sl in sls]
        lhs = [jnp.concatenate([at[:, sl].astype(BF16), rt[:, sl].astype(BF16)], axis=0) for sl in sls]
        g_b = [_dot_nt(lhs[g], block_diag(bt[:, sls[g]].astype(BF16))) for g in groups]
        g_k = [_dot_nt(lhs[g], block_diag(kt[:, sls[g]].astype(BF16))) for g in groups]
        a_ab = [jnp.where(strict, g_b[g][0:CHUNK], 0.0) for g in groups]
        a_rb = [jnp.where(incl, g_b[g][CHUNK:], 0.0) for g in groups]
        a_k = [jnp.where(jnp.concatenate([strict, incl], axis=0), g_k[g], 0.0) for g in groups]
        a_v = [bdmm(a_k[g], v_g[g]) for g in groups]
        yield
        inv = [jnp.where(eye, 1.0, 0.0) + jnp.where(pair, a_ab[g], 0.0) for g in groups]
        for m in lvl_masks:
            low = [bdmm(inv[g], jnp.where(m, a_ab[g], 0.0)) for g in groups]
            yield
            inv = [inv[g] + bdmm(low[g], inv[g]) for g in groups]
            yield
        bkp = [jnp.concatenate([bp[:, sl].astype(BF16), kp[:, sl].astype(BF16)], axis=0) for sl in sls]
        return dict(rows=rows, lhs=lhs, v_g=v_g, a_rb=a_rb, a_v=a_v, inv=inv, bkp=bkp, p_last=p_last)

    def state_dependent(ctx):
        s_q = [_dot_nt(ctx["lhs"][g], s_ref[g].astype(BF16)) for g in groups]
        yield
        u = [bdmm(ctx["inv"][g], s_q[g][0:CHUNK] + ctx["a_v"][g][0:CHUNK]) for g in groups]
        yield
        y = [s_q[g][CHUNK:] + ctx["a_v"][g][CHUNK:] + bdmm(ctx["a_rb"][g], u[g]) for g in groups]
        upd = [_dot_tn(jnp.concatenate([u[g].astype(BF16), ctx["v_g"][g]], axis=0), ctx["bkp"][g]) for g in groups]
        for g in groups:
            o_s[ctx["rows"], sls[g]] = y[g]
            s_ref[g] = s_ref[g] * ctx["p_last"][:, sls[g]] + jnp.where(bd_mask, upd[g], 0.0)

    def emit_interleaved(first, second):
        result, live = None, [g for g in (first, second) if g is not None]
        while live:
            for gen in list(live):
                try:
                    next(gen)
                except StopIteration as stop:
                    live.remove(gen)
                    if gen is first:
                        result = stop.value
        return result

    n_chunks = tb // CHUNK
    ctx = emit_interleaved(state_free(0), None)
    for c in range(n_chunks):
        nxt = state_free(c + 1) if c + 1 < n_chunks else None
        ctx_next = emit_interleaved(nxt, state_dependent(ctx))
        ctx = ctx_next

    o = o_s[...]
    mean = _head_sum(o, ones_bd) * (1.0 / HEAD_DIM)
    oc = o - mean
    var = _head_sum(oc * oc, ones_bd) * (1.0 / HEAD_DIM)
    o = oc * lax.rsqrt(var + RWKV_LN_EPS) * lng_ref[...] + lnb_ref[...]
    rr = r_s[...]
    bonus = _head_sum(rr * k_s[...] * rk_ref[...], ones_bd)
    o = o + bonus * v_s[...]
    y_ref[0] = (o * gate).astype(y_ref.dtype)


def _rwkv(cols, mu, w0, wup, a0, aup, gup, kk, ka, rk, lng, lnb):
    b, lp, _ = cols.shape
    tb = TIME_BLOCK
    par = lambda a: pl.BlockSpec(a.shape, lambda b_, j: (0, 0))
    params = (mu, w0, wup, a0, aup, gup, kk, ka, rk, lng, lnb)
    return pl.pallas_call(
        _rwkv_body,
        grid=(b, lp // tb),
        in_specs=[pl.BlockSpec((1, tb, RWKV_COLS), lambda b_, j: (b_, j, 0))] + [par(a) for a in params],
        out_specs=pl.BlockSpec((1, tb, D_MODEL), lambda b_, j: (b_, j, 0)),
        out_shape=jax.ShapeDtypeStruct((b, lp, D_MODEL), BF16),
        scratch_shapes=[pltpu.VMEM((tb + HALO, RWKV_COLS), F32)]
        + [pltpu.VMEM((tb, D_MODEL), F32) for _ in range(7)]
        + [pltpu.VMEM((D_MODEL // GROUP_W, GROUP_W, GROUP_W), F32),
           pltpu.VMEM((BD_SLOTS, GROUP_W, GROUP_W), BF16)],
        compiler_params=_cparams("parallel", "arbitrary"),
        name="rwkv7_branch",
    )(cols, *params)


def _gelu_tanh(x):
    return 0.5 * x * (1.0 + jnp.tanh(math.sqrt(2.0 / math.pi) * (x + 0.044715 * (x * x * x))))


def _rglru_body(pad, gb_ref, xr_ref, cw_ref, cb_ref, gxw_ref, gxb_ref, gaw_ref, gab_ref, lam_ref,
                y_ref, ext_ref, hc_ref):
    tb = xr_ref.shape[1]
    j = pl.program_id(1)

    @pl.when(j == 0)
    def _():
        hc_ref[...] = jnp.zeros(hc_ref.shape, F32)

    _shift_halo(ext_ref, xr_ref[0], j == 0)
    xf = jnp.broadcast_to(cb_ref[...], (tb, D_MODEL))
    for k in range(CONV_K):
        xf = xf + cw_ref[k:k + 1, :] * ext_ref[pl.ds(HALO - (CONV_K - 1) + k, tb), :]

    gx, ga = [], []
    for hblk in range(LRU_BLOCKS):
        xb = xf[:, hblk * LRU_BLOCK:(hblk + 1) * LRU_BLOCK].astype(BF16)
        gx.append(_dot(xb, gxw_ref[hblk]))
        ga.append(_dot(xb, gaw_ref[hblk]))
    gate_x = _sigmoid(jnp.concatenate(gx, axis=1) + gxb_ref[...])
    gate_a = _sigmoid(jnp.concatenate(ga, axis=1) + gab_ref[...])
    log_a = -LRU_C * gate_a * _softplus(-lam_ref[...])
    a = jnp.exp(log_a)
    u = jnp.sqrt(1.0 - jnp.exp(2.0 * log_a)) * (gate_x * xf)
    row = lax.broadcasted_iota(jnp.int32, (tb, 1), 0)
    u = jnp.where(j * tb + row >= pad, u, 0.0)

    a = a.reshape(tb // SUBLANES, SUBLANES, D_MODEL)
    u = u.reshape(tb // SUBLANES, SUBLANES, D_MODEL)
    sub = lax.broadcasted_iota(jnp.int32, (1, SUBLANES, 1), 1)
    d = 1
    while d < SUBLANES:
        keep = sub >= d
        a_sh = jnp.where(keep, pltpu.roll(a, d, 1), 1.0)
        u_sh = jnp.where(keep, pltpu.roll(u, d, 1), 0.0)
        u = u + a * u_sh
        a = a * a_sh
        d *= 2
    h = hc_ref[...]
    hs = []
    for grp in range(tb // SUBLANES):
        hs.append(a[grp] * h + u[grp])
        h = hs[-1][SUBLANES - 1:SUBLANES, :]
    hc_ref[...] = h
    y_ref[0] = (jnp.concatenate(hs, axis=0) * _gelu_tanh(gb_ref[0])).astype(y_ref.dtype)


def _rglru(proj, cw, cb, gxw, gxb, gaw, gab, lam, pad):
    b, lp, _ = proj.shape
    tb = TIME_BLOCK
    par2 = lambda a: pl.BlockSpec(a.shape, lambda b_, j: (0, 0))
    par3 = lambda a: pl.BlockSpec(a.shape, lambda b_, j: (0, 0, 0))
    return pl.pallas_call(
        functools.partial(_rglru_body, pad),
        grid=(b, lp // tb),
        in_specs=[pl.BlockSpec((1, tb, D_MODEL), lambda b_, j: (b_, j, 0)),
                  pl.BlockSpec((1, tb, D_MODEL), lambda b_, j: (b_, j, 1)),
                  par2(cw), par2(cb), par3(gxw), par2(gxb), par3(gaw), par2(gab), par2(lam)],
        out_specs=pl.BlockSpec((1, tb, D_MODEL), lambda b_, j: (b_, j, 0)),
        out_shape=jax.ShapeDtypeStruct((b, lp, D_MODEL), BF16),
        scratch_shapes=[pltpu.VMEM((tb + HALO, D_MODEL), F32), pltpu.VMEM((1, D_MODEL), F32)],
        compiler_params=_cparams("parallel", "arbitrary"),
        name="rglru_branch",
    )(proj, proj, cw, cb, gxw, gxb, gaw, gab, lam)


def _proj_ln_body(n_in, with_router, *refs):
    xs = refs[:n_in]
    ws = refs[n_in:2 * n_in]
    h_ref, g_ref, b_ref = refs[2 * n_in:2 * n_in + 3]
    rest = refs[2 * n_in + 3:]
    mix = _dot(xs[0][0], ws[0][...])
    for x_ref, w_ref in zip(xs[1:], ws[1:]):
        mix = mix + _dot(x_ref[0], w_ref[...])
    out = _layer_norm(DEEPNORM_ALPHA * h_ref[0] + mix, g_ref[...], b_ref[...])
    if not with_router:
        o_ref, ob_ref = rest
        ob_ref[0] = out.astype(BF16)
    else:
        wr_ref, o_ref, gate_ref, dest_ref, cnt_ref, base_ref = rest
        tm = out.shape[0]
        first = (pl.program_id(0) == 0) & (pl.program_id(1) == 0)

        @pl.when(first)
        def _():
            base_ref[...] = jnp.zeros(base_ref.shape, F32)

        o_hi = out.astype(BF16)
        o_lo = (out - o_hi.astype(F32)).astype(BF16)
        wr = wr_ref[...]
        w_hi = wr.astype(BF16)
        w_lo = (wr - w_hi.astype(F32)).astype(BF16)
        logits = _dot(o_hi, w_hi) + (_dot(o_lo, w_hi) + _dot(o_hi, w_lo))
        lane = lax.broadcasted_iota(jnp.int32, logits.shape, 1)
        logits = jnp.where(lane < N_EXPERTS, logits, -jnp.inf)
        m1 = jnp.max(logits, axis=-1, keepdims=True)
        i1 = jnp.min(jnp.where(logits == m1, lane, LANE), axis=-1, keepdims=True)
        rest_l = jnp.where(lane == i1, -jnp.inf, logits)
        m2 = jnp.max(rest_l, axis=-1, keepdims=True)
        i2 = jnp.min(jnp.where(rest_l == m2, lane, LANE), axis=-1, keepdims=True)
        e2 = jnp.exp(m2 - m1)
        gate_ref[0] = jnp.where(lane == 0, 1.0 / (1.0 + e2), 0.0) + jnp.where(lane == 1, e2 / (1.0 + e2), 0.0)
        hot1 = lane == i1
        hot2 = lane == i2
        both = jnp.where(hot1 | hot2, 1.0, 0.0)
        ts = tm // RANK_SPLIT
        tri_r = lax.broadcasted_iota(jnp.int32, (ts, ts), 0)
        tri_c = lax.broadcasted_iota(jnp.int32, (ts, ts), 1)
        tri = jnp.where(tri_c < tri_r, 1.0, 0.0).astype(BF16)
        base = base_ref[...]
        before = []
        for q in range(RANK_SPLIT):
            sub = both[q * ts:(q + 1) * ts]
            before.append(_dot(tri, sub.astype(BF16)) + base)
            base = base + jnp.sum(sub, axis=0, keepdims=True)
        before = jnp.concatenate(before, axis=0)
        rank1 = jnp.sum(jnp.where(hot1, before, 0.0), axis=-1, keepdims=True).astype(jnp.int32)
        rank2 = jnp.sum(jnp.where(hot2, before, 0.0), axis=-1, keepdims=True).astype(jnp.int32)
        dest_ref[0] = (jnp.where(lane == 0, i1, 0) + jnp.where(lane == 1, i2, 0)
                       + jnp.where(lane == 2, rank1, 0) + jnp.where(lane == 3, rank2, 0))
        base_ref[...] = base
        cnt_ref[...] = base.astype(jnp.int32)
    o_ref[0] = out


def _proj_ln(xs, ws, h, g, bb, tm, w_router=None):
    b, lp, _ = h.shape
    n_in = len(xs)
    row = lambda w: pl.BlockSpec((1, tm, w), lambda b_, i: (b_, i, 0))
    par = lambda a: pl.BlockSpec(a.shape, lambda b_, i: (0, 0))
    in_specs = [row(x.shape[2]) for x in xs] + [par(w) for w in ws] + [row(D_MODEL), par(g), par(bb)]
    args = list(xs) + list(ws) + [h, g, bb]
    if w_router is None:
        out_specs = [row(D_MODEL), row(D_MODEL)]
        out_shape = [jax.ShapeDtypeStruct((b, lp, D_MODEL), F32), jax.ShapeDtypeStruct((b, lp, D_MODEL), BF16)]
        scratch = []
        sem = ("parallel", "parallel")
    else:
        in_specs.append(par(w_router))
        args.append(w_router)
        out_specs = [row(D_MODEL), row(LANE), row(LANE), pl.BlockSpec((1, LANE), lambda b_, i: (0, 0))]
        out_shape = [jax.ShapeDtypeStruct((b, lp, D_MODEL), F32), jax.ShapeDtypeStruct((b, lp, LANE), F32),
                     jax.ShapeDtypeStruct((b, lp, LANE), jnp.int32), jax.ShapeDtypeStruct((1, LANE), jnp.int32)]
        scratch = [pltpu.VMEM((1, LANE), F32)]
        sem = ("arbitrary", "arbitrary")
    return pl.pallas_call(
        functools.partial(_proj_ln_body, n_in, w_router is not None),
        grid=(b, lp // tm),
        in_specs=in_specs, out_specs=out_specs, out_shape=out_shape, scratch_shapes=scratch,
        compiler_params=_cparams(*sem),
        name="out_proj_layernorm",
    )(*args)


def _ffn_body(pad, xb_ref, h_ref, wg_ref, wu_ref, wd_ref, g_ref, b_ref, o_ref, ob_ref, acc_ref):
    f = pl.program_id(2)
    tm = xb_ref.shape[1]

    @pl.when(f == 0)
    def _():
        acc_ref[...] = jnp.zeros(acc_ref.shape, F32)

    x = xb_ref[0]
    act = _silu(_dot(x, wg_ref[...])) * _dot(x, wu_ref[...])
    acc_ref[...] += _dot(act.astype(BF16), wd_ref[...])

    @pl.when(f == pl.num_programs(2) - 1)
    def _():
        out = _layer_norm(DEEPNORM_ALPHA * h_ref[0] + acc_ref[...], g_ref[...], b_ref[...])
        if pad:
            pos = pl.program_id(1) * tm + lax.broadcasted_iota(jnp.int32, (tm, 1), 0)
            out = jnp.where(pos >= pad, out, 0.0)
        o_ref[0] = out
        ob_ref[0] = out.astype(BF16)


def _ffn(xb, h, w_gu, w_down, g, bb, tm, zero_pad):
    b, lp, _ = h.shape
    nf = D_FF // FF_TILE
    tm = tm // FFN_ROW_SPLIT
    row = pl.BlockSpec((1, tm, D_MODEL), lambda b_, i, f: (b_, i, 0))
    par = lambda a: pl.BlockSpec(a.shape, lambda b_, i, f: (0, 0))
    return pl.pallas_call(
        functools.partial(_ffn_body, zero_pad),
        grid=(b, lp // tm, nf),
        in_specs=[row, row,
                  pl.BlockSpec((D_MODEL, FF_TILE), lambda b_, i, f: (0, f)),
                  pl.BlockSpec((D_MODEL, FF_TILE), lambda b_, i, f: (0, f + nf)),
                  pl.BlockSpec((FF_TILE, D_MODEL), lambda b_, i, f: (f, 0)),
                  par(g), par(bb)],
        out_specs=[row, row],
        out_shape=[jax.ShapeDtypeStruct((b, lp, D_MODEL), F32), jax.ShapeDtypeStruct((b, lp, D_MODEL), BF16)],
        scratch_shapes=[pltpu.VMEM((tm, D_MODEL), F32)],
        compiler_params=_cparams("parallel", "parallel", "arbitrary"),
        name="swiglu_layernorm",
    )(xb, h, w_gu, w_gu, w_down, g, bb)


def _row_copy(src, dst, sem):
    return pltpu.make_async_copy(src, dst, sem)


def _dispatch_body(tg, ztile_ref, nz_ref, dest_ref, x_ref, o_hbm, zbuf, xrow, sem, zsem):
    tm = x_ref.shape[1]

    @pl.when(pl.program_id(0) == 0)
    def _():
        zbuf[...] = jnp.zeros(zbuf.shape, F32)
        for phase in ("start", "wait"):
            for k in range(ztile_ref.shape[0]):
                @pl.when(k < nz_ref[0])
                def _():
                    copy = pltpu.make_async_copy(zbuf, o_hbm.at[pl.ds(ztile_ref[k] * tg, tg)], zsem)
                    copy.start() if phase == "start" else copy.wait()

    xrow[...] = x_ref[0].reshape(tm, SUBLANES, LANE)

    def issue(r, c):
        for k in range(2):
            _row_copy(xrow.at[pl.ds(r, 1)], o_hbm.at[pl.ds(dest_ref[0, 0, 2 * r + k], 1)], sem).start(priority=k)
        return c

    lax.fori_loop(0, tm, issue, 0, unroll=8)
    for k in range(2):
        _row_copy(xrow, o_hbm.at[pl.ds(0, tm)], sem).wait()


def _dispatch(h, dest, ztile, nz, n_tiles, tg):
    b, lp, _ = h.shape
    tm = dest.shape[2] // 2
    nt = lp // tm
    grid_spec = pltpu.PrefetchScalarGridSpec(
        num_scalar_prefetch=2,
        grid=(b * nt,),
        in_specs=[pl.BlockSpec((1, 1, 2 * tm), lambda i, zt, nz_: (i, 0, 0), memory_space=pltpu.SMEM),
                  pl.BlockSpec((1, tm, D_MODEL), lambda i, zt, nz_: (i // nt, i % nt, 0))],
        out_specs=pl.BlockSpec(memory_space=pl.ANY),
        scratch_shapes=[pltpu.VMEM((tg,) + ROW_TILE, F32), pltpu.VMEM((tm,) + ROW_TILE, F32),
                        pltpu.SemaphoreType.DMA(()), pltpu.SemaphoreType.DMA(())],
    )
    return pl.pallas_call(
        functools.partial(_dispatch_body, tg),
        grid_spec=grid_spec,
        out_shape=jax.ShapeDtypeStruct((n_tiles * tg,) + ROW_TILE, F32),
        compiler_params=_cparams("arbitrary"),
        name="moe_dispatch",
    )(ztile, nz, dest, h)


def _experts_body(exp_ref, used_ref, x_ref, wg_ref, wu_ref, wd_ref, o_ref, acc_ref):
    t = pl.program_id(0)
    f = pl.program_id(1)
    last = f == pl.num_programs(1) - 1

    @pl.when(used_ref[t] > 0)
    def _():
        @pl.when(f == 0)
        def _():
            acc_ref[...] = jnp.zeros(acc_ref.shape, F32)

        x = x_ref[...].reshape(acc_ref.shape).astype(BF16)
        act = _silu(_dot(x, wg_ref[0])) * _dot(x, wu_ref[0])
        acc_ref[...] += _dot(act.astype(BF16), wd_ref[0])

        @pl.when(last)
        def _():
            o_ref[...] = acc_ref[...].reshape(o_ref.shape)

    @pl.when((used_ref[t] == 0) & last)
    def _():
        o_ref[...] = jnp.zeros(o_ref.shape, F32)


def _experts(xs, exp, used, w_gu, w_down, tg):
    tf = EXPERT_FF_TILE
    nf = D_FF // tf
    n_tiles = exp.shape[0]
    ff = lambda t, f, used_: jnp.where(used_[t] > 0, f, nf - 1)
    grid_spec = pltpu.PrefetchScalarGridSpec(
        num_scalar_prefetch=2,
        grid=(n_tiles, nf),
        in_specs=[pl.BlockSpec((tg,) + ROW_TILE, lambda t, f, exp_, used_: (t, 0, 0)),
                  pl.BlockSpec((1, D_MODEL, tf), lambda t, f, exp_, used_: (exp_[t], 0, ff(t, f, used_))),
                  pl.BlockSpec((1, D_MODEL, tf), lambda t, f, exp_, used_: (exp_[t], 0, ff(t, f, used_) + nf)),
                  pl.BlockSpec((1, tf, D_MODEL), lambda t, f, exp_, used_: (exp_[t], ff(t, f, used_), 0))],
        out_specs=pl.BlockSpec((tg,) + ROW_TILE, lambda t, f, exp_, used_: (t, 0, 0)),
        scratch_shapes=[pltpu.VMEM((tg, D_MODEL), F32)],
    )
    return pl.pallas_call(
        _experts_body,
        grid_spec=grid_spec,
        out_shape=jax.ShapeDtypeStruct(xs.shape, F32),
        compiler_params=_cparams("arbitrary", "arbitrary"),
        name="moe_experts",
    )(exp, used, xs, w_gu, w_gu, w_down)


def _combine_body(row0, nt, dest_ref, dest_next_ref, gate_ref, g_ref, b_ref, h_hbm, y_hbm, o_ref,
                  buf_ref, h_buf, sem, h_sem):
    tc = o_ref.shape[1]
    i = pl.program_id(0)
    n = pl.num_programs(0)
    cur = i % 2

    def h_copy(tile, slot):
        return pltpu.make_async_copy(h_hbm.at[tile // nt, pl.ds(row0 + (tile % nt) * tc, tc)],
                                     h_buf.at[slot], h_sem.at[slot])

    def start_tile(tile, slot, rows_ref):
        h_copy(tile, slot).start()

        def issue(r, c):
            for k in range(2):
                _row_copy(y_hbm.at[pl.ds(rows_ref[0, 0, 2 * r + k], 1)], buf_ref.at[slot, k, pl.ds(r, 1)],
                          sem.at[slot]).start(priority=k)
            return c

        lax.fori_loop(0, tc, issue, 0, unroll=8)

    @pl.when(i == 0)
    def _():
        start_tile(i, cur, dest_ref)

    @pl.when(i + 1 < n)
    def _():
        start_tile(i + 1, 1 - cur, dest_next_ref)

    for k in range(2):
        _row_copy(y_hbm.at[pl.ds(0, tc)], buf_ref.at[cur, k], sem.at[cur]).wait()
    h_copy(i, cur).wait()
    gates = gate_ref[0]
    y = (buf_ref[cur, 0].reshape(tc, D_MODEL) * gates[:, 0:1] + buf_ref[cur, 1].reshape(tc, D_MODEL) * gates[:, 1:2])
    o_ref[0] = _layer_norm(DEEPNORM_ALPHA * h_buf[cur] + y, g_ref[...], b_ref[...])


def _combine(ys, dest, gates, h, g, bb, row0, seq):
    b = h.shape[0]
    tc = dest.shape[2] // 2
    nt = seq // tc
    par = lambda a: pl.BlockSpec(a.shape, lambda i: (0, 0))
    n_steps = b * nt
    return pl.pallas_call(
        functools.partial(_combine_body, row0, nt),
        grid=(n_steps,),
        in_specs=[pl.BlockSpec((1, 1, 2 * tc), lambda i: (i, 0, 0), memory_space=pltpu.SMEM),
                  pl.BlockSpec((1, 1, 2 * tc), lambda i: (jnp.minimum(i + 1, n_steps - 1), 0, 0),
                               memory_space=pltpu.SMEM),
                  pl.BlockSpec((1, tc, LANE), lambda i: (i // nt, i % nt, 0)),
                  par(g), par(bb), pl.BlockSpec(memory_space=pl.ANY), pl.BlockSpec(memory_space=pl.ANY)],
        out_specs=pl.BlockSpec((1, tc, D_MODEL), lambda i: (i // nt, i % nt, 0)),
        out_shape=jax.ShapeDtypeStruct((b, seq, D_MODEL), F32),
        scratch_shapes=[pltpu.VMEM((2, 2, tc) + ROW_TILE, F32), pltpu.VMEM((2, tc, D_MODEL), F32),
                        pltpu.SemaphoreType.DMA((2,)), pltpu.SemaphoreType.DMA((2,))],
        compiler_params=_cparams("arbitrary"),
        name="moe_combine_layernorm",
    )(dest, dest, gates, g, bb, h, ys)


def _moe(h, gates, dest_lanes, counts, w_gu, w_down, g, bb, tm, row0):
    b, lp, _ = h.shape
    seq = lp - row0
    tc = max(t for t in range(SUBLANES, min(seq, 1024) + 1, SUBLANES) if seq % t == 0)
    tg = EXPERT_TILE
    n_tiles = -(-2 * b * lp // tg) + N_EXPERTS
    cnt = counts[0, :N_EXPERTS]
    tiles = (cnt + tg - 1) // tg
    ends = jnp.cumsum(tiles)
    group_row0 = (ends - tiles) * tg
    hot = dest_lanes[:, :, 0:2, None] == jnp.arange(N_EXPERTS, dtype=jnp.int32)
    dest = jnp.sum(jnp.where(hot, group_row0, 0), axis=-1) + dest_lanes[:, :, 2:4]
    dest_frames = dest[:, row0:].reshape(b * (seq // tc), 1, 2 * tc)
    dest = dest.reshape(b * (lp // tm), 1, 2 * tm)
    t = jnp.arange(n_tiles, dtype=jnp.int32)
    used = (t < ends[-1]).astype(jnp.int32)
    exp = jnp.minimum(jnp.searchsorted(ends, jnp.minimum(t, ends[-1] - 1), side="right"),
                      N_EXPERTS - 1).astype(jnp.int32)
    never = jnp.int32(n_tiles)
    tail = ends[-1] + jnp.arange(N_EXPERTS, dtype=jnp.int32)
    ztile = jnp.sort(jnp.concatenate([jnp.where(tiles > 0, ends - 1, never), jnp.where(tail < n_tiles, tail, never)]))
    nz = jnp.sum(ztile < never).astype(jnp.int32).reshape(1)
    xs = _dispatch(h, dest, ztile.astype(jnp.int32), nz, n_tiles, tg)
    ys = _experts(xs, exp, used, w_gu, w_down, tg)
    return _combine(ys, dest_frames, gates[:, row0:], h, g, bb, row0, seq)


def _rep(v):
    return jnp.repeat(v.astype(F32), HEAD_DIM)[None, :]


def _row(v):
    return v.astype(F32).reshape(1, -1)


def kernel(x, meta, ev_w_in, ev_conv_w, ev_conv_b, ev_dt_bias, ev_a_log, ev_d_skip, ev_ssm_norm, ev_shift_mu, ev_w0, ev_w_up, ev_a0, ev_a_up, ev_g_up, ev_k_k, ev_k_a, ev_r_k, ev_lnx_g, ev_lnx_b, ev_w_out, ev_ln1_g, ev_ln1_b, ev_ffn_w_gu, ev_ffn_w_down, ev_ln2_g, ev_ln2_b, od_w_in, od_conv_w, od_conv_b, od_gx_w, od_gx_b, od_ga_w, od_ga_b, od_lambda, od_w_out, od_ln1_g, od_ln1_b, od_router, od_exp_w_gu, od_exp_w_down, od_ln2_g, od_ln2_b):
    b, seq, d = x.shape
    assert d == D_MODEL
    l = seq + N_META
    pad = (-l) % CHUNK
    lp = l + pad
    assert lp % TIME_BLOCK == 0, lp
    tm = _row_tile(lp)

    h = jnp.concatenate([jnp.zeros((b, pad, d), x.dtype),
                         jnp.broadcast_to(meta.astype(x.dtype)[None], (b, N_META, d)), x], axis=1)

    i = 0
    w_in = ev_w_in[i]
    o1 = D_MODEL
    o2 = o1 + SSM_CONV_DIM
    o3 = o2 + N_HEADS
    z, hb = _mm(h, w_in[:, :o1].astype(BF16), F32, tm)
    xbc = _mm(hb, w_in[:, o1:o2].astype(BF16), F32, tm)
    dtx = _mm(hb, jnp.repeat(w_in[:, o2:o3], HEAD_DIM, axis=1).astype(BF16), F32, tm)
    cols = _mm(hb, w_in[:, o3:].astype(BF16), F32, tm)

    y_a = _ssd(z, xbc, dtx, ev_conv_w[i], _row(ev_conv_b[i]), _rep(ev_dt_bias[i]), _rep(ev_a_log[i]),
               _rep(ev_d_skip[i]), _row(ev_ssm_norm[i]), pad)

    zeros64 = jnp.zeros((64, D_MODEL), F32)
    wup = jnp.concatenate([ev_w_up[i], zeros64], axis=0).astype(BF16)
    aup = jnp.concatenate([zeros64, ev_a_up[i]], axis=0).astype(BF16)
    y_b = _rwkv(cols, _row(ev_shift_mu[i]), _row(ev_w0[i]), wup, _row(ev_a0[i]), aup, ev_g_up[i].astype(BF16),
                _row(ev_k_k[i]), _row(ev_k_a[i]), _row(ev_r_k[i]), _row(ev_lnx_g[i]), _row(ev_lnx_b[i]))

    w_out = ev_w_out[i].astype(BF16)
    h, hb = _proj_ln([y_a, y_b], [w_out[:D_MODEL], w_out[D_MODEL:]], h, _row(ev_ln1_g[i]), _row(ev_ln1_b[i]), tm)
    h, hb = _ffn(hb, h, ev_ffn_w_gu[i].astype(BF16), ev_ffn_w_down[i].astype(BF16),
                 _row(ev_ln2_g[i]), _row(ev_ln2_b[i]), tm, pad)

    proj = _mm(hb, od_w_in[i].astype(BF16), F32, tm)
    y_c = _rglru(proj, od_conv_w[i], _row(od_conv_b[i]), od_gx_w[i].astype(BF16), _row(od_gx_b[i]),
                 od_ga_w[i].astype(BF16), _row(od_ga_b[i]), _row(od_lambda[i]), pad)
    w_router = jnp.pad(od_router[i], ((0, 0), (0, LANE - N_EXPERTS)))
    h, gates, dest, counts = _proj_ln([y_c], [od_w_out[i].astype(BF16)], h, _row(od_ln1_g[i]), _row(od_ln1_b[i]), tm,
                                      w_router=w_router)
    return _moe(h, gates, dest, counts, od_exp_w_gu[i].astype(BF16), od_exp_w_down[i].astype(BF16),
                _row(od_ln2_g[i]), _row(od_ln2_b[i]), tm, pad + N_META)
```
